```python
import functools
import jax, jax.numpy as jnp
from jax import lax
import numpy as np

D_MODEL = 1024
BATCH = 4
SEQ = 8192
DEPTH = 1
DEC_BATCH = 128
DEC_SEQ = 1
PAST_LEN = 8192
PAGE_SIZE = 128

N_HEADS = 8
HEAD_DIM = 64
ATTN_WIDTH = N_HEADS * HEAD_DIM
Q_BLOCK = 128
FORGET_BIAS = 1.0
ATTN_SCALE = HEAD_DIM ** -0.5
SGU_GROUPS = 8
SGU_GROUP_DIM = 64
SGU_WIDTH = SGU_GROUPS * SGU_GROUP_DIM
CHUNK = 128
N_EXPERTS = 64
TOP_K = 8
N_EXPERT_GROUPS = 8
TOP_K_GROUPS = 4
EXPERTS_PER_GROUP = N_EXPERTS // N_EXPERT_GROUPS
D_EXPERT = 256
D_SHARED = 256
ROUTED_SCALE = 2.5
EXPERT_BLOCK = 128
NORM_EPS = 1e-6
IN_SPLITS = (ATTN_WIDTH, 2 * ATTN_WIDTH, 3 * ATTN_WIDTH, 3 * ATTN_WIDTH + N_HEADS,
             3 * ATTN_WIDTH + N_HEADS + SGU_WIDTH, 3 * ATTN_WIDTH + N_HEADS + 2 * SGU_WIDTH,
             3 * ATTN_WIDTH + N_HEADS + 2 * SGU_WIDTH + D_MODEL)
IN_WIDTH = 3 * ATTN_WIDTH + N_HEADS + 2 * SGU_WIDTH + 2 * D_MODEL

kernel_name = 'fox_gmlp_moe_adaln_hybrid_step'


def rms_norm(x, g):
    xf = x.astype(jnp.float32)
    y = xf * lax.rsqrt(jnp.mean(xf * xf, axis=-1, keepdims=True) + NORM_EPS)
    return (y * g.astype(jnp.float32)).astype(x.dtype)


def layer_norm(x, g, b):
    xf = x.astype(jnp.float32)
    xc = xf - jnp.mean(xf, axis=-1, keepdims=True)
    y = xc * lax.rsqrt(jnp.mean(xc * xc, axis=-1, keepdims=True) + NORM_EPS)
    return (y * g.astype(jnp.float32) + b.astype(jnp.float32)).astype(x.dtype)


def swiglu(x, w_gate, w_up, w_down):
    return (jax.nn.silu(x @ w_gate) * (x @ w_up)) @ w_down


def fox_prompt(q, k, v, logf):
    B, S, H, HD = q.shape
    Ft = jnp.cumsum(logf, axis=1).transpose(0, 2, 1)
    key_pos = jnp.arange(S)

    def one_block(i):
        q0 = i * Q_BLOCK
        qi = lax.dynamic_slice_in_dim(q, q0, Q_BLOCK, axis=1)
        Fi = lax.dynamic_slice_in_dim(Ft, q0, Q_BLOCK, axis=2)
        s = jnp.einsum('bqhd,bkhd->bhqk', qi, k).astype(jnp.float32) * ATTN_SCALE
        s = s + (Fi[..., None] - Ft[:, :, None, :])
        causal = key_pos[None, :] <= (q0 + jnp.arange(Q_BLOCK))[:, None]
        p = jax.nn.softmax(jnp.where(causal, s, -jnp.inf), axis=-1).astype(v.dtype)
        return jnp.einsum('bhqk,bkhd->bqhd', p, v)

    o = lax.map(one_block, jnp.arange(S // Q_BLOCK))
    return o.transpose(1, 0, 2, 3, 4).reshape(B, S, H, HD)


def fox_decode(q, k, v, logf, *, cache_k, cache_v, cache_logf, page_table, layer):
    Sd = q.shape[1]
    causal = jnp.tril(jnp.ones((Sd, Sd), bool))

    def one_seq(args):
        qb, kb, vb, lfb, pages = args
        kp = cache_k[layer, pages].reshape(-1, N_HEADS, HEAD_DIM)
        vp = cache_v[layer, pages].reshape(-1, N_HEADS, HEAD_DIM)
        lfp = cache_logf[layer, pages].reshape(-1, N_HEADS).astype(jnp.float32)
        F_p = jnp.cumsum(lfp, axis=0)
        F_n = F_p[-1] + jnp.cumsum(lfb, axis=0)
        s_p = jnp.einsum('qhd,khd->hqk', qb, kp).astype(jnp.float32) * ATTN_SCALE
        s_p = s_p + (F_n.T[:, :, None] - F_p.T[:, None, :])
        s_n = jnp.einsum('qhd,khd->hqk', qb, kb).astype(jnp.float32) * ATTN_SCALE
        s_n = jnp.where(causal, s_n + (F_n.T[:, :, None] - F_n.T[:, None, :]), -jnp.inf)
        p = jax.nn.softmax(jnp.concatenate([s_p, s_n], axis=-1), axis=-1).astype(vb.dtype)
        n_past = kp.shape[0]
        return (jnp.einsum('hqk,khd->qhd', p[..., :n_past], vp)
                + jnp.einsum('hqk,khd->qhd', p[..., n_past:], vb))

    return lax.map(one_seq, (q, k, v, logf, page_table))


def spatial_mix(vn, w_spatial, b_spatial):
    B, L, G, C = vn.shape
    lc = min(L, CHUNK)
    w = jnp.where(jnp.tril(jnp.ones((lc, lc), bool)), w_spatial[:, :lc, :lc], 0)
    vc = vn.reshape(B, L // lc, lc, G, C)
    mixed = jnp.einsum('gts,bnsgc->bntgc', w, vc) + b_spatial[:, :lc].T[None, None, :, :, None]
    return mixed.reshape(B, L, G, C)


def route(h, w_router, b_router):
    T = h.shape[0]
    scores = jax.nn.sigmoid((h @ w_router).astype(jnp.float32))
    biased = scores + b_router.astype(jnp.float32)
    grp_score = lax.top_k(biased.reshape(T, N_EXPERT_GROUPS, EXPERTS_PER_GROUP), 2)[0].sum(-1)
    _, gidx = lax.top_k(grp_score, TOP_K_GROUPS)
    gmask = jnp.any(gidx[:, :, None] == jnp.arange(N_EXPERT_GROUPS)[None, None, :], axis=1)
    cand = jnp.where(jnp.repeat(gmask, EXPERTS_PER_GROUP, axis=1), biased, -jnp.inf)
    _, idx = lax.top_k(cand, TOP_K)
    w = jnp.take_along_axis(scores, idx, axis=-1)
    w = w / jnp.sum(w, axis=-1, keepdims=True) * ROUTED_SCALE
    return idx, w


def routed_experts(h, idx, wts, w_exp_gate, w_exp_up, w_exp_down):
    T, D = h.shape
    tk = T * TOP_K
    flat_e = idx.reshape(tk)
    order = jnp.argsort(flat_e)
    e_sorted = flat_e[order]
    tok_sorted = (order // TOP_K).astype(jnp.int32)
    w_sorted = wts.reshape(tk)[order].astype(h.dtype)
    counts = jnp.zeros((N_EXPERTS,), jnp.int32).at[flat_e].add(1)
    padded = (counts + EXPERT_BLOCK - 1) // EXPERT_BLOCK * EXPERT_BLOCK
    pad_end = jnp.cumsum(padded)
    pad_start = pad_end - padded
    grp_start = jnp.cumsum(counts) - counts
    dest = pad_start[e_sorted] + jnp.arange(tk, dtype=jnp.int32) - grp_start[e_sorted]
    n_blk = -(-tk // EXPERT_BLOCK) + N_EXPERTS
    tok_buf = jnp.full((n_blk * EXPERT_BLOCK,), T, jnp.int32).at[dest].set(tok_sorted)
    w_buf = jnp.zeros((n_blk * EXPERT_BLOCK,), h.dtype).at[dest].set(w_sorted)
    blk_e = jnp.minimum(jnp.searchsorted(pad_end, jnp.arange(n_blk, dtype=jnp.int32) * EXPERT_BLOCK,
                                         side='right'), N_EXPERTS - 1)
    h_pad = jnp.concatenate([h, jnp.zeros((1, D), h.dtype)], axis=0)

    def step(y, blk):
        tok, w, e = blk
        out = swiglu(h_pad[tok], w_exp_gate[e], w_exp_up[e], w_exp_down[e])
        return y.at[tok].add(out * w[:, None]), None

    y, _ = lax.scan(step, jnp.zeros((T + 1, D), h.dtype),
                    (tok_buf.reshape(n_blk, EXPERT_BLOCK), w_buf.reshape(n_blk, EXPERT_BLOCK), blk_e))
    return y[:T]


def decoder_layer(x, c, attend, w_ada, b_ada, g_norm_mix, g_norm_ffn, w_in, b_forget, g_q, g_k,
                  g_vnorm, b_vnorm, w_spatial, b_spatial, w_branch_a, w_branch_b, w_out,
                  w_router, b_router, w_exp_gate, w_exp_up, w_exp_down, w_sh_gate, w_sh_up, w_sh_down):
    B, L, D = x.shape
    mod = (jax.nn.silu(c) @ w_ada + b_ada)[:, None, :]
    shift1, scale1, gate1, shift2, scale2, gate2 = jnp.split(mod, 6, axis=-1)
    h = rms_norm(x, g_norm_mix) * (1 + scale1) + shift1
    q, k, v, f_logit, u, vg, ga, gb = jnp.split(h @ w_in, IN_SPLITS, axis=-1)
    q = rms_norm(q.reshape(B, L, N_HEADS, HEAD_DIM), g_q)
    k = rms_norm(k.reshape(B, L, N_HEADS, HEAD_DIM), g_k)
    v = v.reshape(B, L, N_HEADS, HEAD_DIM)
    logf = jax.nn.log_sigmoid(f_logit.astype(jnp.float32) + b_forget.astype(jnp.float32))
    o_a = attend(q, k, v, logf).reshape(B, L, ATTN_WIDTH)
    vn = layer_norm(jax.nn.gelu(vg), g_vnorm, b_vnorm)
    mixed = spatial_mix(vn.reshape(B, L, SGU_GROUPS, SGU_GROUP_DIM), w_spatial, b_spatial)
    o_b = jax.nn.gelu(u) * mixed.reshape(B, L, SGU_WIDTH)
    merged = jax.nn.sigmoid(ga) * (o_a @ w_branch_a) + jax.nn.sigmoid(gb) * (o_b @ w_branch_b)
    x = x + gate1 * (merged @ w_out)
    h2 = (rms_norm(x, g_norm_ffn) * (1 + scale2) + shift2).reshape(B * L, D)
    idx, wts = route(h2, w_router, b_router)
    ffn = swiglu(h2, w_sh_gate, w_sh_up, w_sh_down) + routed_experts(h2, idx, wts, w_exp_gate, w_exp_up, w_exp_down)
    x = x + gate2 * ffn.reshape(B, L, D)
    return x, k, v, logf, vn


def setup_inputs(seed: int = 0) -> dict:
    key = jax.random.key(seed)
    keys = iter(jax.random.split(key, 40))

    def nrm(shape, scale):
        return scale * jax.random.normal(next(keys), shape, jnp.float32)

    D = D_MODEL
    n_pages = PAST_LEN // PAGE_SIZE
    n_used = DEC_BATCH * n_pages
    n_pool = n_used + n_used // 4 + 1
    page_table = jax.random.permutation(next(keys), n_pool)[:n_used].reshape(DEC_BATCH, n_pages).astype(jnp.int32)
    return {
        'x_prompt': nrm((BATCH, SEQ, D), 1.0),
        'x_sample': nrm((DEC_BATCH, DEC_SEQ, D), 1.0),
        'c_prompt': nrm((BATCH, D), 1.0),
        'c_sample': nrm((DEC_BATCH, D), 1.0),
        'cache_k': nrm((DEPTH, n_pool, PAGE_SIZE, N_HEADS, HEAD_DIM), 1.0),
        'cache_v': nrm((DEPTH, n_pool, PAGE_SIZE, N_HEADS, HEAD_DIM), 1.0),
        'cache_logf': jax.nn.log_sigmoid(nrm((DEPTH, n_pool, PAGE_SIZE, N_HEADS), 1.0) + FORGET_BIAS),
        'page_table': page_table,
        'w_ada': nrm((DEPTH, D, 6 * D), 0.5 * D ** -0.5),
        'b_ada': nrm((DEPTH, 6 * D), 0.02),
        'g_norm_mix': 1.0 + nrm((DEPTH, D), 0.1),
        'g_norm_ffn': 1.0 + nrm((DEPTH, D), 0.1),
        'w_in': nrm((DEPTH, D, IN_WIDTH), D ** -0.5),
        'b_forget': FORGET_BIAS + nrm((DEPTH, N_HEADS), 0.1),
        'g_q': 1.0 + nrm((DEPTH, HEAD_DIM), 0.1),
        'g_k': 1.0 + nrm((DEPTH, HEAD_DIM), 0.1),
        'g_vnorm': 1.0 + nrm((DEPTH, SGU_WIDTH), 0.1),
        'b_vnorm': nrm((DEPTH, SGU_WIDTH), 0.02),
        'w_spatial': nrm((DEPTH, SGU_GROUPS, CHUNK, CHUNK), CHUNK ** -0.5),
        'b_spatial': 1.0 + nrm((DEPTH, SGU_GROUPS, CHUNK), 0.1),
        'w_branch_a': nrm((DEPTH, ATTN_WIDTH, D), ATTN_WIDTH ** -0.5),
        'w_branch_b': nrm((DEPTH, SGU_WIDTH, D), SGU_WIDTH ** -0.5),
        'w_out': nrm((DEPTH, D, D), D ** -0.5),
        'w_router': nrm((DEPTH, D, N_EXPERTS), D ** -0.5),
        'b_router': nrm((DEPTH, N_EXPERTS), 0.01),
        'w_exp_gate': nrm((DEPTH, N_EXPERTS, D, D_EXPERT), D ** -0.5),
        'w_exp_up': nrm((DEPTH, N_EXPERTS, D, D_EXPERT), D ** -0.5),
        'w_exp_down': nrm((DEPTH, N_EXPERTS, D_EXPERT, D), D_EXPERT ** -0.5),
        'w_sh_gate': nrm((DEPTH, D, D_SHARED), D ** -0.5),
        'w_sh_up': nrm((DEPTH, D, D_SHARED), D ** -0.5),
        'w_sh_down': nrm((DEPTH, D_SHARED, D), D_SHARED ** -0.5),
    }


def reference(x_prompt, x_sample, c_prompt, c_sample, cache_k, cache_v, cache_logf, page_table,
              w_ada, b_ada, g_norm_mix, g_norm_ffn, w_in, b_forget, g_q, g_k, g_vnorm, b_vnorm,
              w_spatial, b_spatial, w_branch_a, w_branch_b, w_out, w_router, b_router,
              w_exp_gate, w_exp_up, w_exp_down, w_sh_gate, w_sh_up, w_sh_down):
    y_prompt, y_sample = x_prompt, x_sample
    k_p, v_p, lf_p, k_s, v_s, lf_s, vn_s = [], [], [], [], [], [], []
    for layer in range(DEPTH):
        lw = (w_ada[layer], b_ada[layer], g_norm_mix[layer], g_norm_ffn[layer], w_in[layer],
              b_forget[layer], g_q[layer], g_k[layer], g_vnorm[layer], b_vnorm[layer],
              w_spatial[layer], b_spatial[layer], w_branch_a[layer], w_branch_b[layer], w_out[layer],
              w_router[layer], b_router[layer], w_exp_gate[layer], w_exp_up[layer], w_exp_down[layer],
              w_sh_gate[layer], w_sh_up[layer], w_sh_down[layer])
        y_prompt, kp, vp, lfp, _ = decoder_layer(y_prompt, c_prompt, fox_prompt, *lw)
        attend_sample = functools.partial(fox_decode, cache_k=cache_k, cache_v=cache_v,
                                          cache_logf=cache_logf, page_table=page_table, layer=layer)
        y_sample, ks, vs, lfs, vns = decoder_layer(y_sample, c_sample, attend_sample, *lw)
        k_p.append(kp); v_p.append(vp); lf_p.append(lfp)
        k_s.append(ks); v_s.append(vs); lf_s.append(lfs); vn_s.append(vns)
    return (y_prompt, y_sample, jnp.stack(k_p), jnp.stack(v_p), jnp.stack(lf_p),
            jnp.stack(k_s), jnp.stack(v_s), jnp.stack(lf_s), jnp.stack(vn_s))
```

```python
import functools

import jax
import jax.numpy as jnp
from jax import lax
from jax.experimental import pallas as pl
from jax.experimental.pallas import tpu as pltpu

F32 = jnp.float32
BF16 = jnp.bfloat16
I32 = jnp.int32

D_MODEL = 1024
N_HEADS = 8
HEAD_DIM = 64
ATTN_WIDTH = N_HEADS * HEAD_DIM
SGU_GROUPS = 8
SGU_WIDTH = 512
CHUNK = 128
N_EXPERTS = 64
TOP_K = 8
N_EXPERT_GROUPS = 8
TOP_K_GROUPS = 4
EXPERTS_PER_GROUP = 8
D_EXPERT = 256
D_SHARED = 256
ROUTED_SCALE = 2.5
NORM_EPS = 1e-6
ATTN_SCALE = HEAD_DIM ** -0.5
PAGE = 128
LANES = 128
PAGE_ROW = PAGE * N_HEADS
VMEM_LIMIT = 56 * 1024 * 1024

_dot = functools.partial(jnp.dot, preferred_element_type=F32)


def _dot_nt(a, b):
    return lax.dot_general(a, b, (((1,), (1,)), ((), ())), preferred_element_type=F32)


def _sigmoid(x):
    return 1.0 / (1.0 + jnp.exp(-x))


def _gelu(x):
    return 0.5 * x * (1.0 + jnp.tanh(0.7978845608028654 * (x + 0.044715 * (x * x * x))))


def _split3(x):
    hi = x.astype(BF16)
    r1 = x - hi.astype(F32)
    mid = r1.astype(BF16)
    lo = (r1 - mid.astype(F32)).astype(BF16)
    return hi, mid, lo


def _dot3_left(m_bf16, x):
    hi, mid, lo = _split3(x)
    return _dot(m_bf16, hi) + _dot(m_bf16, mid) + _dot(m_bf16, lo)


def _dot3_right(x, m_bf16):
    hi, mid, lo = _split3(x)
    return _dot(hi, m_bf16) + _dot(mid, m_bf16) + _dot(lo, m_bf16)


def _params(*sem):
    return pltpu.CompilerParams(dimension_semantics=sem, vmem_limit_bytes=VMEM_LIMIT)


def _const_spec(shape):
    zeros = (0,) * len(shape)
    return pl.BlockSpec(shape, lambda *_: zeros)


def _ada_kernel(c_ref, w_ref, b_ref, o_ref):
    c = c_ref[...]
    a = c * _sigmoid(c)
    o_ref[...] = jnp.dot(a, w_ref[...], preferred_element_type=F32,
                         precision=lax.Precision.HIGHEST) + b_ref[...]


def _ada(c, w_ada, b_ada):
    m, d = c.shape
    n = w_ada.shape[1]
    tn = 1024
    return pl.pallas_call(
        _ada_kernel,
        grid=(n // tn,),
        in_specs=[pl.BlockSpec((m, d), lambda j: (0, 0)),
                  pl.BlockSpec((d, tn), lambda j: (0, j)),
                  pl.BlockSpec((1, tn), lambda j: (0, j))],
        out_specs=pl.BlockSpec((m, tn), lambda j: (0, j)),
        out_shape=jax.ShapeDtypeStruct((m, n), F32),
        compiler_params=_params("arbitrary"),
        name="ada",
    )(c, w_ada, b_ada.reshape(1, n))


def _mix_in_kernel(x_ref, mod_ref, gmix_ref, wq_ref, wk_ref, wv_ref, wf_ref, wu_ref, wvg_ref,
                   wga_ref, wgb_ref, bf_ref, gq_ref, gk_ref, gvn_ref, bvn_ref, bd_ref,
                   wsp_ref, bsp_ref, wbb_ref, ltri_ref,
                   q_ref, kf_ref, vf_ref, kb_ref, vb_ref, lf_ref, fc_ref, mb_ref, sga_ref, vn_ref,
                   carry_sc):
    tl = x_ref.shape[1]
    x = x_ref[0]
    shift1 = mod_ref[0, :, 0:D_MODEL]
    scale1 = mod_ref[0, :, D_MODEL:2 * D_MODEL]
    ms = jnp.mean(x * x, axis=-1, keepdims=True)
    h = x * lax.rsqrt(ms + NORM_EPS) * gmix_ref[...] * (1.0 + scale1) + shift1
    hb = h.astype(BF16)
    bd = bd_ref[...]

    def head_norm(z, g):
        ss = _dot((z * z).astype(BF16), bd) * (1.0 / HEAD_DIM)
        return z * lax.rsqrt(ss + NORM_EPS) * g

    qn = head_norm(_dot(hb, wq_ref[...]), gq_ref[...]) * ATTN_SCALE
    q_ref[0] = qn.astype(BF16)
    kn = head_norm(_dot(hb, wk_ref[...]), gk_ref[...])
    kf_ref[0] = kn
    kb_ref[0] = kn.astype(BF16)
    v = _dot(hb, wv_ref[...])
    vf_ref[0] = v
    vb_ref[0] = v.astype(BF16)

    zf = _dot(hb, wf_ref[...]) + bf_ref[...]
    lf = jnp.minimum(zf, 0.0) - jnp.log(1.0 + jnp.exp(-jnp.abs(zf)))
    lf_ref[0] = lf[:, :N_HEADS]

    @pl.when(pl.program_id(1) == 0)
    def _():
        carry_sc[...] = jnp.zeros_like(carry_sc)

    fc = _dot3_left(ltri_ref[...], lf) + carry_sc[...]
    fc_ref[0] = fc[:, :N_HEADS]
    carry_sc[...] = fc[tl - 1:tl, :]

    gu = _gelu(_dot(hb, wu_ref[...]))
    gv = _gelu(_dot(hb, wvg_ref[...]))
    mu = jnp.mean(gv, axis=-1, keepdims=True)
    gc = gv - mu
    var = jnp.mean(gc * gc, axis=-1, keepdims=True)
    vn = gc * lax.rsqrt(var + NORM_EPS) * gvn_ref[...] + bvn_ref[...]
    vn_ref[0] = vn
    vnb = vn.astype(BF16)
    lane = lax.broadcasted_iota(I32, (CHUNK, LANES), 1)
    low = lane < (LANES // 2)
    zero = jnp.zeros((CHUNK, LANES), BF16)
    rows = []
    for c in range(tl // CHUNK):
        pieces = []
        for j in range(SGU_WIDTH // LANES):
            vp = vnb[c * CHUNK:(c + 1) * CHUNK, j * LANES:(j + 1) * LANES]
            mixed = (_dot(wsp_ref[2 * j], jnp.where(low, vp, zero))
                     + _dot(wsp_ref[2 * j + 1], jnp.where(low, zero, vp)) + bsp_ref[j])
            pieces.append(mixed)
        rows.append(jnp.concatenate(pieces, axis=1))
    mixed = rows[0] if len(rows) == 1 else jnp.concatenate(rows, axis=0)
    ob = (gu * mixed).astype(BF16)
    mb = _sigmoid(_dot(hb, wgb_ref[...])) * _dot(ob, wbb_ref[...])
    mb_ref[0] = mb.astype(BF16)
    sga_ref[0] = _sigmoid(_dot(hb, wga_ref[...])).astype(BF16)


def _mix_in(x, mod, wts, tl):
    bx, l, d = x.shape
    tlm = tl if mod.shape[1] > 1 else 1
    grid = (bx, l // tl)
    row = lambda b, i: (b, i, 0)
    mod_map = row if tlm > 1 else (lambda b, i: (b, 0, 0))
    names = ("gmix", "wq", "wk", "wv", "wf", "wu", "wvg", "wga", "wgb", "bf", "gq", "gk", "gvn",
             "bvn", "bd", "wsp", "bsp", "wbb", "ltri")
    consts = [wts[n] for n in names]
    out_widths = [(ATTN_WIDTH, BF16), (ATTN_WIDTH, F32), (ATTN_WIDTH, F32), (ATTN_WIDTH, BF16),
                  (ATTN_WIDTH, BF16), (N_HEADS, F32), (N_HEADS, F32), (D_MODEL, BF16),
                  (D_MODEL, BF16), (SGU_WIDTH, F32)]
    return pl.pallas_call(
        _mix_in_kernel,
        grid=grid,
        in_specs=[pl.BlockSpec((1, tl, d), row), pl.BlockSpec((1, tlm, 6 * d), mod_map)]
                 + [_const_spec(c.shape) for c in consts],
        out_specs=[pl.BlockSpec((1, tl, w), row) for w, _ in out_widths],
        out_shape=[jax.ShapeDtypeStruct((bx, l, w), dt) for w, dt in out_widths],
        scratch_shapes=[pltpu.VMEM((1, LANES), F32)],
        compiler_params=_params("arbitrary", "arbitrary"),
        name="mix_in",
    )(x, mod, *consts)


def _attn_kernel(q_ref, k_ref, v_ref, fq_ref, fk_ref, o_ref, m_sc, l_sc, acc_sc):
    tq = q_ref.shape[1]
    tk = k_ref.shape[1]
    qi = pl.program_id(1)
    ki = pl.program_id(2)

    @pl.when(ki == 0)
    def _():
        m_sc[...] = jnp.full_like(m_sc, -jnp.inf)
        l_sc[...] = jnp.zeros_like(l_sc)
        acc_sc[...] = jnp.zeros_like(acc_sc)

    def step(masked):
        lane = lax.broadcasted_iota(I32, (tq, LANES), 1)
        low = lane < HEAD_DIM
        if masked:
            causal = (lax.broadcasted_iota(I32, (tq, tk), 0) >= lax.broadcasted_iota(I32, (tq, tk), 1))
        for j in range(ATTN_WIDTH // LANES):
            sl = slice(j * LANES, (j + 1) * LANES)
            qp = q_ref[0, :, sl]
            kp = k_ref[0, :, sl]
            vp = v_ref[0, :, sl]
            zero = jnp.zeros_like(qp)
            alphas = []
            pvs = []
            for t in range(2):
                hd = 2 * j + t
                qh = jnp.where(low, qp, zero) if t == 0 else jnp.where(low, zero, qp)
                s = _dot_nt(qh, kp) + (fq_ref[0, :, hd:hd + 1] - fk_ref[0, hd:hd + 1, :])
                if masked:
                    s = jnp.where(causal, s, -jnp.inf)
                m_prev = m_sc[hd]
                m_new = jnp.maximum(m_prev, jnp.max(s, axis=-1, keepdims=True))
                alpha = jnp.exp(m_prev - m_new)
                p = jnp.exp(s - m_new)
                l_sc[hd] = alpha * l_sc[hd] + jnp.sum(p, axis=-1, keepdims=True)
                m_sc[hd] = m_new
                alphas.append(alpha)
                pvs.append(_dot(p.astype(BF16), vp))
            acc_sc[j] = (acc_sc[j] * jnp.where(low, alphas[0], alphas[1])
                         + jnp.where(low, pvs[0], pvs[1]))

    @pl.when(ki < qi)
    def _():
        step(False)

    @pl.when(ki == qi)
    def _():
        step(True)
        lane = lax.broadcasted_iota(I32, (tq, LANES), 1)
        low = lane < HEAD_DIM
        for j in range(ATTN_WIDTH // LANES):
            inv = jnp.where(low, 1.0 / l_sc[2 * j], 1.0 / l_sc[2 * j + 1])
            o_ref[0, :, j * LANES:(j + 1) * LANES] = (acc_sc[j] * inv).astype(o_ref.dtype)


def _attn_prompt(q, k, v, fc, fr, tq):
    b, s, w = q.shape
    nq = s // tq
    kv_map = lambda bi, qi, ki: (bi, jnp.minimum(ki, qi), 0)
    return pl.pallas_call(
        _attn_kernel,
        grid=(b, nq, nq),
        in_specs=[pl.BlockSpec((1, tq, w), lambda bi, qi, ki: (bi, qi, 0)),
                  pl.BlockSpec((1, tq, w), kv_map),
                  pl.BlockSpec((1, tq, w), kv_map),
                  pl.BlockSpec((1, tq, N_HEADS), lambda bi, qi, ki: (bi, qi, 0)),
                  pl.BlockSpec((1, N_HEADS, tq), lambda bi, qi, ki: (bi, 0, jnp.minimum(ki, qi)))],
        out_specs=pl.BlockSpec((1, tq, w), lambda bi, qi, ki: (bi, qi, 0)),
        out_shape=jax.ShapeDtypeStruct((b, s, w), BF16),
        scratch_shapes=[pltpu.VMEM((N_HEADS, tq, 1), F32), pltpu.VMEM((N_HEADS, tq, 1), F32),
                        pltpu.VMEM((w // LANES, tq, LANES), F32)],
        compiler_params=_params("arbitrary", "arbitrary", "arbitrary"),
        name="attn_prompt",
    )(q, k, v, fc, fr)


def _decay_kernel(pt_ref, lf_hbm, umat_ref, tmat_ref, upage_ref, o_ref, buf, sem):
    g = pl.program_id(0)
    rows = buf.shape[0]

    def copy(r):
        return pltpu.make_async_copy(lf_hbm.at[pl.ds(pt_ref[g * rows + r], 1)],
                                     buf.at[pl.ds(r, 1)], sem.at[0])

    def issue(r, c):
        copy(r).start()
        return c

    def wait(r, c):
        copy(r).wait()
        return c

    lax.fori_loop(0, rows, issue, 0)
    lax.fori_loop(0, rows, wait, 0)
    hi, mid, lo = _split3(buf[...])
    umat = umat_ref[...]
    tmat = tmat_ref[...]
    later_in_page = _dot(hi, umat) + _dot(mid, umat) + _dot(lo, umat)
    page_total = _dot(hi, tmat) + _dot(mid, tmat) + _dot(lo, tmat)
    o_ref[...] = later_in_page + _dot3_left(upage_ref[...], page_total)


def _decay_bias(page_table, lf_rows, seqs_per_step):
    nb, n_pages = page_table.shape
    rows = seqs_per_step * n_pages
    lane = jnp.arange(PAGE_ROW)
    same_head = (lane[:, None] % N_HEADS) == (lane[None, :] % N_HEADS)
    umat = (same_head & (lane[:, None] // N_HEADS > lane[None, :] // N_HEADS)).astype(BF16)
    tmat = same_head.astype(BF16)
    r = jnp.arange(rows)
    upage = ((r[:, None] // n_pages == r[None, :] // n_pages) & (r[None, :] > r[:, None])).astype(BF16)
    grid_spec = pltpu.PrefetchScalarGridSpec(
        num_scalar_prefetch=1,
        grid=(nb // seqs_per_step,),
        in_specs=[pl.BlockSpec(memory_space=pl.ANY),
                  pl.BlockSpec((PAGE_ROW, PAGE_ROW), lambda g, pt: (0, 0)),
                  pl.BlockSpec((PAGE_ROW, PAGE_ROW), lambda g, pt: (0, 0)),
                  pl.BlockSpec((rows, rows), lambda g, pt: (0, 0))],
        out_specs=pl.BlockSpec((rows, PAGE_ROW), lambda g, pt: (g, 0)),
        scratch_shapes=[pltpu.VMEM((rows, PAGE_ROW), F32), pltpu.SemaphoreType.DMA((1,))],
    )
    return pl.pallas_call(
        _decay_kernel,
        grid_spec=grid_spec,
        out_shape=jax.ShapeDtypeStruct((nb * n_pages, PAGE_ROW), F32),
        compiler_params=_params("arbitrary"),
        name="decay_bias",
    )(page_table.reshape(-1), lf_rows, umat, tmat, upage)


def _attn_decode_kernel(pt_ref, q_ref, kn_ref, vn_ref, lfn_ref, bias_ref, ck_ref, cv_ref, o_ref,
                        m_sc, l_sc, acc_sc):
    j = pl.program_id(1)
    nj = pl.num_programs(1)

    @pl.when(j == 0)
    def _():
        m_sc[...] = jnp.full_like(m_sc, -jnp.inf)
        l_sc[...] = jnp.zeros_like(l_sc)
        acc_sc[...] = jnp.zeros_like(acc_sc)

    q = q_ref[0]
    kp = ck_ref[0, 0].reshape(PAGE_ROW, HEAD_DIM).astype(BF16)
    vp = cv_ref[0, 0].reshape(PAGE_ROW, HEAD_DIM).astype(BF16)
    s = _dot_nt(q.astype(BF16), kp)
    own = (lax.broadcasted_iota(I32, (N_HEADS, PAGE_ROW), 1) % N_HEADS
           == lax.broadcasted_iota(I32, (N_HEADS, PAGE_ROW), 0))
    s = jnp.where(own, s + bias_ref[0, 0] + lfn_ref[0], -jnp.inf)
    m_prev = m_sc[...]
    m_new = jnp.maximum(m_prev, jnp.max(s, axis=-1, keepdims=True))
    alpha = jnp.exp(m_prev - m_new)
    p = jnp.exp(s - m_new)
    l_sc[...] = alpha * l_sc[...] + jnp.sum(p, axis=-1, keepdims=True)
    acc_sc[...] = alpha * acc_sc[...] + _dot(p.astype(BF16), vp)
    m_sc[...] = m_new

    @pl.when(j == nj - 1)
    def _():
        s_n = jnp.sum(q * kn_ref[0], axis=-1, keepdims=True)
        m_prev = m_sc[...]
        m_new = jnp.maximum(m_prev, s_n)
        alpha = jnp.exp(m_prev - m_new)
        p_n = jnp.exp(s_n - m_new)
        l = alpha * l_sc[...] + p_n
        o_ref[0] = (alpha * acc_sc[...] + p_n * vn_ref[0]) / l


def _attn_decode(page_table, q, k_new, v_new, lf_new, bias, cache_k, cache_v):
    nb, n_pages = page_table.shape
    tok = lambda b, j, pt: (b, 0, 0)
    page = lambda b, j, pt: (0, pt[b * n_pages + n_pages - 1 - j], 0, 0, 0)
    grid_spec = pltpu.PrefetchScalarGridSpec(
        num_scalar_prefetch=1,
        grid=(nb, n_pages),
        in_specs=[pl.BlockSpec((1, N_HEADS, HEAD_DIM), tok),
                  pl.BlockSpec((1, N_HEADS, HEAD_DIM), tok),
                  pl.BlockSpec((1, N_HEADS, HEAD_DIM), tok),
                  pl.BlockSpec((1, N_HEADS, 1), tok),
                  pl.BlockSpec((1, 1, 1, PAGE_ROW), lambda b, j, pt: (b, n_pages - 1 - j, 0, 0)),
                  pl.BlockSpec((1, 1, PAGE, N_HEADS, HEAD_DIM), page),
                  pl.BlockSpec((1, 1, PAGE, N_HEADS, HEAD_DIM), page)],
        out_specs=pl.BlockSpec((1, N_HEADS, HEAD_DIM), tok),
        scratch_shapes=[pltpu.VMEM((N_HEADS, 1), F32), pltpu.VMEM((N_HEADS, 1), F32),
                        pltpu.VMEM((N_HEADS, HEAD_DIM), F32)],
    )
    return pl.pallas_call(
        _attn_decode_kernel,
        grid_spec=grid_spec,
        out_shape=jax.ShapeDtypeStruct((nb, N_HEADS, HEAD_DIM), F32),
        compiler_params=_params("arbitrary", "arbitrary"),
        name="attn_decode",
    )(page_table.reshape(-1), q, k_new, v_new, lf_new,
      bias.reshape(nb, n_pages, 1, PAGE_ROW), cache_k, cache_v)


def _mix_out_kernel(x_ref, oa_ref, sga_ref, mb_ref, mod_ref, wba_ref, wo_ref, gffn_ref,
                    wrh_ref, wrl_ref, wsg_ref, wsu_ref, wsd_ref,
                    h2_ref, lg_ref, base_ref):
    d = D_MODEL
    x = x_ref[0]
    gate1 = mod_ref[0, :, 2 * d:3 * d]
    shift2 = mod_ref[0, :, 3 * d:4 * d]
    scale2 = mod_ref[0, :, 4 * d:5 * d]
    gate2 = mod_ref[0, :, 5 * d:6 * d]
    merged = sga_ref[0].astype(F32) * _dot(oa_ref[0], wba_ref[...]) + mb_ref[0].astype(F32)
    x1 = x + gate1 * _dot(merged.astype(BF16), wo_ref[...])
    ms = jnp.mean(x1 * x1, axis=-1, keepdims=True)
    h2 = x1 * lax.rsqrt(ms + NORM_EPS) * gffn_ref[...] * (1.0 + scale2) + shift2
    h2_ref[0] = h2
    hb = h2.astype(BF16)
    hl = (h2 - hb.astype(F32)).astype(BF16)
    lg_ref[0] = _dot(hb, wrh_ref[...]) + (_dot(hb, wrl_ref[...]) + _dot(hl, wrh_ref[...]))
    g = _dot(hb, wsg_ref[...])
    u = _dot(hb, wsu_ref[...])
    a = (g * _sigmoid(g) * u).astype(BF16)
    base_ref[0] = x1 + gate2 * _dot(a, wsd_ref[...])


def _mix_out(x, oa, sga, mb, mod, wts, tl):
    bx, l, d = x.shape
    tlm = tl if mod.shape[1] > 1 else 1
    row = lambda b, i: (b, i, 0)
    mod_map = row if tlm > 1 else (lambda b, i: (b, 0, 0))
    names = ("wba", "wo", "gffn", "wrh", "wrl", "wsg", "wsu", "wsd")
    consts = [wts[n] for n in names]
    return pl.pallas_call(
        _mix_out_kernel,
        grid=(bx, l // tl),
        in_specs=[pl.BlockSpec((1, tl, d), row), pl.BlockSpec((1, tl, ATTN_WIDTH), row),
                  pl.BlockSpec((1, tl, d), row), pl.BlockSpec((1, tl, d), row),
                  pl.BlockSpec((1, tlm, 6 * d), mod_map)] + [_const_spec(c.shape) for c in consts],
        out_specs=[pl.BlockSpec((1, tl, d), row), pl.BlockSpec((1, tl, LANES), row),
                   pl.BlockSpec((1, tl, d), row)],
        out_shape=[jax.ShapeDtypeStruct((bx, l, d), F32), jax.ShapeDtypeStruct((bx, l, LANES), F32),
                   jax.ShapeDtypeStruct((bx, l, d), F32)],
        compiler_params=_params("arbitrary", "arbitrary"),
        name="mix_out",
    )(x, oa, sga, mb, mod, *consts)


def _route_kernel(lg_ref, b_ref, idx_ref, w_ref):
    tt = lg_ref.shape[1]
    epg = EXPERTS_PER_GROUP
    ninf = -jnp.inf
    iota = lax.broadcasted_iota(I32, (epg, tt), 0)
    sc = []
    biased = []
    gscore = []
    for g in range(N_EXPERT_GROUPS):
        s = _sigmoid(lg_ref[g * epg:(g + 1) * epg, :])
        bz = s + b_ref[g * epg:(g + 1) * epg, :]
        m1 = jnp.max(bz, axis=0, keepdims=True)
        first = jnp.min(jnp.where(bz == m1, iota, epg), axis=0, keepdims=True)
        m2 = jnp.max(jnp.where(iota == first, ninf, bz), axis=0, keepdims=True)
        sc.append(s)
        biased.append(bz)
        gscore.append(m1 + m2)
    cand = []
    for g in range(N_EXPERT_GROUPS):
        rank = jnp.zeros((1, tt), I32)
        for o in range(N_EXPERT_GROUPS):
            if o == g:
                continue
            beats = (gscore[o] >= gscore[g]) if o < g else (gscore[o] > gscore[g])
            rank = rank + beats.astype(I32)
        cand.append(jnp.where(rank < TOP_K_GROUPS, biased[g], ninf))
    ws = []
    for k in range(TOP_K):
        mx = cand[0]
        for g in range(1, N_EXPERT_GROUPS):
            mx = jnp.maximum(mx, cand[g])
        mx = jnp.max(mx, axis=0, keepdims=True)
        fi = jnp.where(cand[0] == mx, iota, N_EXPERTS)
        for g in range(1, N_EXPERT_GROUPS):
            fi = jnp.minimum(fi, jnp.where(cand[g] == mx, iota + g * epg, N_EXPERTS))
        fi = jnp.min(fi, axis=0, keepdims=True)
        wk = jnp.zeros((epg, tt), F32)
        for g in range(N_EXPERT_GROUPS):
            hit = (iota + g * epg) == fi
            wk = wk + jnp.where(hit, sc[g], 0.0)
            cand[g] = jnp.where(hit, ninf, cand[g])
        idx_ref[k:k + 1, :] = fi
        ws.append(jnp.sum(wk, axis=0, keepdims=True))
    tot = ws[0]
    for k in range(1, TOP_K):
        tot = tot + ws[k]
    for k in range(TOP_K):
        w_ref[k:k + 1, :] = ws[k] / tot * ROUTED_SCALE


def _route(logits_t, b_router, tt):
    t = logits_t.shape[1]
    return pl.pallas_call(
        _route_kernel,
        grid=(t // tt,),
        in_specs=[pl.BlockSpec((N_EXPERTS, tt), lambda i: (0, i)),
                  pl.BlockSpec((N_EXPERTS, 1), lambda i: (0, 0))],
        out_specs=[pl.BlockSpec((TOP_K, tt), lambda i: (0, i)), pl.BlockSpec((TOP_K, tt), lambda i: (0, i))],
        out_shape=[jax.ShapeDtypeStruct((TOP_K, t), I32), jax.ShapeDtypeStruct((TOP_K, t), F32)],
        compiler_params=_params("arbitrary"),
        name="route",
    )(logits_t, b_router.reshape(N_EXPERTS, 1))


def _moe_kernel(te_ref, nu_ref, tok_ref, tokn_ref, ws_ref, h_hbm, wg_ref, wu_ref, wd_ref, o_ref,
                xbuf, sem):
    i = pl.program_id(0)
    n = pl.num_programs(0)
    tm = xbuf.shape[1]
    slot = i % 2

    def copy(idx_ref, r, s):
        return pltpu.make_async_copy(h_hbm.at[pl.ds(idx_ref[0, 0, r], 1)],
                                     xbuf.at[s, pl.ds(r, 1)], sem.at[s])

    def issue(idx_ref, s):
        def body(r, c):
            copy(idx_ref, r, s).start()
            return c
        lax.fori_loop(0, tm, body, 0)

    @pl.when(i == 0)
    def _():
        issue(tok_ref, 0)

    @pl.when(i + 1 < n)
    def _():
        issue(tokn_ref, 1 - slot)

    def wait(r, c):
        copy(tok_ref, r, slot).wait()
        return c
    lax.fori_loop(0, tm, wait, 0)

    @pl.when(i < nu_ref[0])
    def _():
        x = xbuf[slot].astype(BF16)
        g = _dot(x, wg_ref[0].astype(BF16))
        u = _dot(x, wu_ref[0].astype(BF16))
        a = (g * _sigmoid(g) * u).astype(BF16)
        o_ref[...] = _dot(a, wd_ref[0].astype(BF16)) * ws_ref[...]

    @pl.when(i >= nu_ref[0])
    def _():
        o_ref[...] = jnp.zeros_like(o_ref)


def _moe(tile_e, n_used, tok_buf, w_buf, h2, w_gate, w_up, w_down, tm):
    n_tiles = tile_e.shape[0]
    d = h2.shape[1]
    tok3 = tok_buf.reshape(n_tiles, 1, tm)
    grid_spec = pltpu.PrefetchScalarGridSpec(
        num_scalar_prefetch=2,
        grid=(n_tiles,),
        in_specs=[pl.BlockSpec((1, 1, tm), lambda i, te, nu: (i, 0, 0), memory_space=pltpu.SMEM),
                  pl.BlockSpec((1, 1, tm), lambda i, te, nu: (jnp.minimum(i + 1, n_tiles - 1), 0, 0),
                               memory_space=pltpu.SMEM),
                  pl.BlockSpec((tm, 1), lambda i, te, nu: (i, 0)),
                  pl.BlockSpec(memory_space=pl.ANY),
                  pl.BlockSpec((1, d, D_EXPERT), lambda i, te, nu: (te[i], 0, 0)),
                  pl.BlockSpec((1, d, D_EXPERT), lambda i, te, nu: (te[i], 0, 0)),
                  pl.BlockSpec((1, D_EXPERT, d), lambda i, te, nu: (te[i], 0, 0))],
        out_specs=pl.BlockSpec((tm, d), lambda i, te, nu: (i, 0)),
        scratch_shapes=[pltpu.VMEM((2, tm, d), F32), pltpu.SemaphoreType.DMA((2,))],
    )
    return pl.pallas_call(
        _moe_kernel,
        grid_spec=grid_spec,
        out_shape=jax.ShapeDtypeStruct((n_tiles * tm, d), F32),
        compiler_params=_params("arbitrary"),
        name="moe_experts",
    )(tile_e, n_used, tok3, tok3, w_buf.reshape(n_tiles * tm, 1), h2, w_gate, w_up, w_down)


def _combine_kernel(pos_ref, posn_ref, eo_hbm, base_ref, mod_ref, y_ref, buf, sem):
    tt = base_ref.shape[1]
    i = pl.program_id(0) * pl.num_programs(1) + pl.program_id(1)
    n = pl.num_programs(0) * pl.num_programs(1)
    slot = i % 2
    rows = tt * TOP_K

    def copy(idx_ref, r, s):
        return pltpu.make_async_copy(eo_hbm.at[pl.ds(idx_ref[0, 0, r], 1)],
                                     buf.at[s, r % TOP_K, pl.ds(r // TOP_K, 1)], sem.at[s])

    def issue(idx_ref, s):
        def body(r, c):
            copy(idx_ref, r, s).start()
            return c
        lax.fori_loop(0, rows, body, 0)

    @pl.when(i == 0)
    def _():
        issue(pos_ref, 0)

    @pl.when(i + 1 < n)
    def _():
        issue(posn_ref, 1 - slot)

    def wait(r, c):
        copy(pos_ref, r, slot).wait()
        return c
    lax.fori_loop(0, rows, wait, 0)

    routed = buf[slot, 0]
    for k in range(1, TOP_K):
        routed = routed + buf[slot, k]
    y_ref[0] = base_ref[0] + mod_ref[0, :, 5 * D_MODEL:6 * D_MODEL] * routed


def _combine(pos, eo, base, mod, tt):
    bx, l, d = base.shape
    nl = l // tt
    n = bx * nl
    pos3 = pos.reshape(n, 1, tt * TOP_K)
    tlm = tt if mod.shape[1] > 1 else 1
    row = lambda b, i: (b, i, 0)
    mod_map = row if tlm > 1 else (lambda b, i: (b, 0, 0))
    return pl.pallas_call(
        _combine_kernel,
        grid=(bx, nl),
        in_specs=[pl.BlockSpec((1, 1, tt * TOP_K), lambda b, i: (b * nl + i, 0, 0), memory_space=pltpu.SMEM),
                  pl.BlockSpec((1, 1, tt * TOP_K), lambda b, i: (jnp.minimum(b * nl + i + 1, n - 1), 0, 0),
                               memory_space=pltpu.SMEM),
                  pl.BlockSpec(memory_space=pl.ANY),
                  pl.BlockSpec((1, tt, d), row),
                  pl.BlockSpec((1, tlm, 6 * d), mod_map)],
        out_specs=pl.BlockSpec((1, tt, d), row),
        out_shape=jax.ShapeDtypeStruct((bx, l, d), F32),
        scratch_shapes=[pltpu.VMEM((2, TOP_K, tt, d), F32), pltpu.SemaphoreType.DMA((2,))],
        compiler_params=_params("arbitrary", "arbitrary"),
        name="moe_combine",
    )(pos3, pos3, eo, base, mod)


def _dispatch_plan(idx_t, w_t, tm):
    k, t = idx_t.shape
    tk = t * k
    flat_e = idx_t.T.reshape(tk)
    flat_w = w_t.T.reshape(tk)
    order = jnp.argsort(flat_e)
    e_sorted = flat_e[order]
    tok_sorted = (order // k).astype(I32)
    counts = jnp.zeros((N_EXPERTS,), I32).at[flat_e].add(1)
    padded = (counts + tm - 1) // tm * tm
    pad_end = jnp.cumsum(padded)
    pad_start = pad_end - padded
    grp_start = jnp.cumsum(counts) - counts
    dest = pad_start[e_sorted] + jnp.arange(tk, dtype=I32) - grp_start[e_sorted]
    n_tiles = -(-tk // tm) + N_EXPERTS
    tok_buf = jnp.zeros((n_tiles * tm,), I32).at[dest].set(tok_sorted, unique_indices=True)
    w_buf = jnp.zeros((n_tiles * tm,), F32).at[dest].set(flat_w[order], unique_indices=True)
    pos = jnp.zeros((tk,), I32).at[order].set(dest, unique_indices=True)
    tile_e = jnp.minimum(jnp.searchsorted(pad_end, jnp.arange(n_tiles, dtype=I32) * tm, side='right'),
                         N_EXPERTS - 1).astype(I32)
    n_used = (pad_end[-1] // tm).astype(I32).reshape(1)
    return tile_e, n_used, tok_buf, w_buf, pos


def _prep_weights(w_in, b_forget, g_q, g_k, g_vnorm, b_vnorm, g_norm_mix, g_norm_ffn,
                  w_branch_a, w_branch_b, w_out, w_router, w_sh_gate, w_sh_up, w_sh_down):
    aw, sw, d = ATTN_WIDTH, SGU_WIDTH, D_MODEL
    o = 3 * aw + N_HEADS
    wf = jnp.zeros((d, LANES), F32).at[:, :N_HEADS].set(w_in[:, 3 * aw:o])
    bf = jnp.zeros((1, LANES), F32).at[0, :N_HEADS].set(b_forget)
    lane = jnp.arange(aw)
    bd = (lane[:, None] // HEAD_DIM == lane[None, :] // HEAD_DIM).astype(BF16)
    wr = jnp.zeros((d, LANES), F32).at[:, :N_EXPERTS].set(w_router)
    wrh = wr.astype(BF16)
    wrl = (wr - wrh.astype(F32)).astype(BF16)
    return dict(
        gmix=g_norm_mix.reshape(1, d), gffn=g_norm_ffn.reshape(1, d),
        wq=w_in[:, 0:aw].astype(BF16), wk=w_in[:, aw:2 * aw].astype(BF16),
        wv=w_in[:, 2 * aw:3 * aw].astype(BF16), wf=wf.astype(BF16), bf=bf,
        wu=w_in[:, o:o + sw].astype(BF16), wvg=w_in[:, o + sw:o + 2 * sw].astype(BF16),
        wga=w_in[:, o + 2 * sw:o + 2 * sw + d].astype(BF16),
        wgb=w_in[:, o + 2 * sw + d:o + 2 * sw + 2 * d].astype(BF16),
        gq=jnp.tile(g_q, N_HEADS).reshape(1, aw), gk=jnp.tile(g_k, N_HEADS).reshape(1, aw),
        gvn=g_vnorm.reshape(1, sw), bvn=b_vnorm.reshape(1, sw), bd=bd,
        wbb=w_branch_b.astype(BF16), wba=w_branch_a.astype(BF16), wo=w_out.astype(BF16),
        wrh=wrh, wrl=wrl, wsg=w_sh_gate.astype(BF16), wsu=w_sh_up.astype(BF16),
        wsd=w_sh_down.astype(BF16))


def _spatial_weights(w_spatial, b_spatial, rows_are_sequences, tl):
    if rows_are_sequences:
        wsp = w_spatial[:, 0, 0][:, None, None] * jnp.eye(CHUNK, dtype=F32)[None]
        b = jnp.broadcast_to(b_spatial[:, 0:1], (SGU_GROUPS, CHUNK))
    else:
        wsp = jnp.where(jnp.tril(jnp.ones((CHUNK, CHUNK), bool)), w_spatial, 0)
        b = b_spatial
    half = LANES // 2
    bsp = jnp.repeat(b.reshape(SGU_GROUPS // 2, 2, CHUNK), half, axis=1)
    bsp = bsp.transpose(0, 2, 1)
    r = jnp.arange(tl)
    ltri = (r[:, None] >= r[None, :]).astype(BF16)
    return dict(wsp=wsp.astype(BF16), bsp=bsp, ltri=ltri)


def _layer(x, mod, attend, wts, w_spatial, b_spatial, b_router, w_exp_gate, w_exp_up, w_exp_down,
           rows_are_sequences, tl, tm, tt_route, tt_comb):
    bx, l, d = x.shape
    wts = dict(wts, **_spatial_weights(w_spatial, b_spatial, rows_are_sequences, tl))
    q, kf, vf, kb, vb, lf, fc, mb, sga, vn = _mix_in(x, mod, wts, tl)
    oa = attend(q, kf, vf, kb, vb, lf, fc)
    h2, logits, base = _mix_out(x, oa, sga, mb, mod, wts, tl)
    t = bx * l
    idx_t, w_t = _route(logits.reshape(t, LANES).T, b_router, tt_route)
    tile_e, n_used, tok_buf, w_buf, pos = _dispatch_plan(idx_t, w_t, tm)
    eo = _moe(tile_e, n_used, tok_buf, w_buf, h2.reshape(t, d), w_exp_gate, w_exp_up, w_exp_down, tm)
    y = _combine(pos, eo, base, mod, tt_comb)
    return y, kf, vf, lf, vn


def kernel(x_prompt, x_sample, c_prompt, c_sample, cache_k, cache_v, cache_logf, page_table, w_ada, b_ada, g_norm_mix, g_norm_ffn, w_in, b_forget, g_q, g_k, g_vnorm, b_vnorm, w_spatial, b_spatial, w_branch_a, w_branch_b, w_out, w_router, b_router, w_exp_gate, w_exp_up, w_exp_down, w_sh_gate, w_sh_up, w_sh_down):
    assert w_ada.shape[0] == 1, "one layer"
    b, s, d = x_prompt.shape
    nb = x_sample.shape[0]
    n_pages = page_table.shape[1]
    n_pool = cache_k.shape[1]

    c_all = jnp.concatenate([c_prompt, c_sample], axis=0)
    pad = (-c_all.shape[0]) % 8
    c_all = jnp.pad(c_all, ((0, pad), (0, 0)))
    mod_all = _ada(c_all, w_ada[0], b_ada[0])
    mod_p = mod_all[:b].reshape(b, 1, 6 * d)
    mod_s = mod_all[b:b + nb].reshape(1, nb, 6 * d)

    wts = _prep_weights(w_in[0], b_forget[0], g_q[0], g_k[0], g_vnorm[0], b_vnorm[0], g_norm_mix[0],
                        g_norm_ffn[0], w_branch_a[0], w_branch_b[0], w_out[0], w_router[0],
                        w_sh_gate[0], w_sh_up[0], w_sh_down[0])
    experts = (w_exp_gate[0], w_exp_up[0], w_exp_down[0])

    tq = min(512, s)

    def attend_prompt(q, kf, vf, kb, vb, lf, fc):
        return _attn_prompt(q, kb, vb, fc, fc.transpose(0, 2, 1), tq)

    def attend_sample(q, kf, vf, kb, vb, lf, fc):
        hd = (nb, N_HEADS, HEAD_DIM)
        bias = _decay_bias(page_table, cache_logf[0].reshape(n_pool, PAGE_ROW), 8)
        o = _attn_decode(page_table, q.astype(F32).reshape(hd), kf.reshape(hd), vf.reshape(hd),
                         lf.reshape(nb, N_HEADS, 1), bias, cache_k, cache_v)
        return o.reshape(1, nb, ATTN_WIDTH).astype(BF16)

    y_s, k_s, v_s, lf_s, vn_s = _layer(x_sample.reshape(1, nb, d), mod_s, attend_sample, wts,
                                       w_spatial[0], b_spatial[0], b_router[0], *experts,
                                       rows_are_sequences=True, tl=nb, tm=32, tt_route=nb, tt_comb=64)
    y_p, k_p, v_p, lf_p, _ = _layer(x_prompt, mod_p, attend_prompt, wts, w_spatial[0], b_spatial[0],
                                    b_router[0], *experts, rows_are_sequences=False,
                                    tl=min(512, s), tm=min(512, s), tt_route=min(1024, b * s),
                                    tt_comb=64)
    hd5 = (1, b, s, N_HEADS, HEAD_DIM)
    sd5 = (1, nb, 1, N_HEADS, HEAD_DIM)
    return (y_p, y_s.reshape(nb, 1, d),
            k_p.reshape(hd5), v_p.reshape(hd5), lf_p.reshape(1, b, s, N_HEADS),
            k_s.reshape(sd5), v_s.reshape(sd5), lf_s.reshape(1, nb, 1, N_HEADS),
            vn_s.reshape(1, nb, 1, SGU_WIDTH))
```

```python
import functools

import jax
import jax.numpy as jnp
from jax import lax
from jax.experimental import pallas as pl
from jax.experimental.pallas import tpu as pltpu

F32 = jnp.float32
BF16 = jnp.bfloat16
I32 = jnp.int32

D_MODEL = 1024
N_HEADS = 8
HEAD_DIM = 64
ATTN_WIDTH = N_HEADS * HEAD_DIM
SGU_GROUPS = 8
SGU_WIDTH = 512
CHUNK = 128
N_EXPERTS = 64
TOP_K = 8
N_EXPERT_GROUPS = 8
TOP_K_GROUPS = 4
EXPERTS_PER_GROUP = 8
D_EXPERT = 256
D_SHARED = 256
ROUTED_SCALE = 2.5
NORM_EPS = 1e-6
ATTN_SCALE = HEAD_DIM ** -0.5
LOG2E = 1.4426950408889634
PAGE = 128
LANES = 128
VMEM_LIMIT = 56 * 1024 * 1024
PAGES_PER_STEP = 8

_dot = functools.partial(jnp.dot, preferred_element_type=F32)


def _dot_nt(a, b):
    return lax.dot_general(a, b, (((1,), (1,)), ((), ())), preferred_element_type=F32)


def _sigmoid(x):
    return 1.0 / (1.0 + jnp.exp(-x))


def _gelu(x):
    return 0.5 * x * (1.0 + jnp.tanh(0.7978845608028654 * (x + 0.044715 * (x * x * x))))


def _split3(x):
    hi = x.astype(BF16)
    r1 = x - hi.astype(F32)
    mid = r1.astype(BF16)
    lo = (r1 - mid.astype(F32)).astype(BF16)
    return hi, mid, lo


def _dot3_left(m_bf16, x):
    hi, mid, lo = _split3(x)
    return _dot(m_bf16, hi) + _dot(m_bf16, mid) + _dot(m_bf16, lo)


def _dot3_right(x, m_bf16):
    hi, mid, lo = _split3(x)
    return _dot(hi, m_bf16) + _dot(mid, m_bf16) + _dot(lo, m_bf16)


def _params(*sem, row_dma=False):
    return pltpu.CompilerParams(dimension_semantics=sem, vmem_limit_bytes=VMEM_LIMIT,
                                disable_bounds_checks=row_dma)


def _const_spec(shape):
    zeros = (0,) * len(shape)
    return pl.BlockSpec(shape, lambda *_: zeros)


def _ada_kernel(c_ref, w_ref, b_ref, o_ref):
    c = c_ref[...]
    a = c * _sigmoid(c)
    o_ref[...] = jnp.dot(a, w_ref[...], preferred_element_type=F32,
                         precision=lax.Precision.HIGHEST) + b_ref[...]


def _ada(c, w_ada, b_ada):
    m, d = c.shape
    n = w_ada.shape[1]
    tn = 1024
    return pl.pallas_call(
        _ada_kernel,
        grid=(n // tn,),
        in_specs=[pl.BlockSpec((m, d), lambda j: (0, 0)),
                  pl.BlockSpec((d, tn), lambda j: (0, j)),
                  pl.BlockSpec((1, tn), lambda j: (0, j))],
        out_specs=pl.BlockSpec((m, tn), lambda j: (0, j)),
        out_shape=jax.ShapeDtypeStruct((m, n), F32),
        compiler_params=_params("arbitrary"),
        name="ada",
    )(c, w_ada, b_ada.reshape(1, n))


def _mix_in_kernel(x_ref, mod_ref, gmix_ref, wq_ref, wk_ref, wv_ref, wf_ref, wu_ref, wvg_ref,
                   wga_ref, wgb_ref, bf_ref, gq_ref, gk_ref, gvn_ref, bvn_ref, bd_ref,
                   wsp_ref, bsp_ref, wbb_ref, ltri_ref,
                   q_ref, kf_ref, vf_ref, kb_ref, vb_ref, lf_ref, fc_ref, mb_ref, sga_ref, vn_ref,
                   carry_sc):
    tl = x_ref.shape[1]
    x = x_ref[0]
    shift1 = mod_ref[0, :, 0:D_MODEL]
    scale1 = mod_ref[0, :, D_MODEL:2 * D_MODEL]
    ms = jnp.mean(x * x, axis=-1, keepdims=True)
    h = x * lax.rsqrt(ms + NORM_EPS) * gmix_ref[...] * (1.0 + scale1) + shift1
    hb = h.astype(BF16)
    bd = bd_ref[...]

    def head_norm(z, g):
        ss = _dot((z * z).astype(BF16), bd) * (1.0 / HEAD_DIM)
        return z * lax.rsqrt(ss + NORM_EPS) * g

    qn = head_norm(_dot(hb, wq_ref[...]), gq_ref[...]) * (ATTN_SCALE * LOG2E)
    q_ref[0] = qn.astype(BF16)
    kn = head_norm(_dot(hb, wk_ref[...]), gk_ref[...])
    kf_ref[0] = kn
    kb_ref[0] = kn.astype(BF16)
    v = _dot(hb, wv_ref[...])
    vf_ref[0] = v
    vb_ref[0] = v.astype(BF16)

    zf = _dot(hb, wf_ref[...]) + bf_ref[...]
    lf = jnp.minimum(zf, 0.0) - jnp.log(1.0 + jnp.exp(-jnp.abs(zf)))
    lf_ref[0] = lf[:, :N_HEADS]

    @pl.when(pl.program_id(1) == 0)
    def _():
        carry_sc[...] = jnp.zeros_like(carry_sc)

    fc = _dot3_left(ltri_ref[...], lf) + carry_sc[...]
    fc_ref[0] = fc[:, :N_HEADS]
    carry_sc[...] = fc[tl - 1:tl, :]

    gu = _gelu(_dot(hb, wu_ref[...]))
    gv = _gelu(_dot(hb, wvg_ref[...]))
    mu = jnp.mean(gv, axis=-1, keepdims=True)
    gc = gv - mu
    var = jnp.mean(gc * gc, axis=-1, keepdims=True)
    vn = gc * lax.rsqrt(var + NORM_EPS) * gvn_ref[...] + bvn_ref[...]
    vn_ref[0] = vn
    vnb = vn.astype(BF16)
    lane = lax.broadcasted_iota(I32, (CHUNK, LANES), 1)
    low = lane < (LANES // 2)
    zero = jnp.zeros((CHUNK, LANES), BF16)
    rows = []
    for c in range(tl // CHUNK):
        pieces = []
        for j in range(SGU_WIDTH // LANES):
            vp = vnb[c * CHUNK:(c + 1) * CHUNK, j * LANES:(j + 1) * LANES]
            mixed = (_dot(wsp_ref[2 * j], jnp.where(low, vp, zero))
                     + _dot(wsp_ref[2 * j + 1], jnp.where(low, zero, vp)) + bsp_ref[j])
            pieces.append(mixed)
        rows.append(jnp.concatenate(pieces, axis=1))
    mixed = rows[0] if len(rows) == 1 else jnp.concatenate(rows, axis=0)
    ob = (gu * mixed).astype(BF16)
    mb = _sigmoid(_dot(hb, wgb_ref[...])) * _dot(ob, wbb_ref[...])
    mb_ref[0] = mb.astype(BF16)
    sga_ref[0] = _sigmoid(_dot(hb, wga_ref[...])).astype(BF16)


def _mix_in(x, mod, wts, tl):
    bx, l, d = x.shape
    tlm = tl if mod.shape[1] > 1 else 1
    grid = (bx, l // tl)
    row = lambda b, i: (b, i, 0)
    mod_map = row if tlm > 1 else (lambda b, i: (b, 0, 0))
    names = ("gmix", "wq", "wk", "wv", "wf", "wu", "wvg", "wga", "wgb", "bf", "gq", "gk", "gvn",
             "bvn", "bd", "wsp", "bsp", "wbb", "ltri")
    consts = [wts[n] for n in names]
    out_widths = [(ATTN_WIDTH, BF16), (ATTN_WIDTH, F32), (ATTN_WIDTH, F32), (ATTN_WIDTH, BF16),
                  (ATTN_WIDTH, BF16), (N_HEADS, F32), (N_HEADS, F32), (D_MODEL, BF16),
                  (D_MODEL, BF16), (SGU_WIDTH, F32)]
    return pl.pallas_call(
        _mix_in_kernel,
        grid=grid,
        in_specs=[pl.BlockSpec((1, tl, d), row), pl.BlockSpec((1, tlm, 6 * d), mod_map)]
                 + [_const_spec(c.shape) for c in consts],
        out_specs=[pl.BlockSpec((1, tl, w), row) for w, _ in out_widths],
        out_shape=[jax.ShapeDtypeStruct((bx, l, w), dt) for w, dt in out_widths],
        scratch_shapes=[pltpu.VMEM((1, LANES), F32)],
        compiler_params=_params("arbitrary", "arbitrary"),
        name="mix_in",
    )(x, mod, *consts)


def _attn_kernel(qi_ref, ki_ref, q_ref, k_ref, v_ref, fq_ref, fk_ref, o_ref, m_sc, l_sc, acc_sc):
    tq = q_ref.shape[1]
    tk = k_ref.shape[1]
    pair = pl.program_id(1)
    qi = qi_ref[pair]
    ki = ki_ref[pair]

    @pl.when(ki == 0)
    def _():
        m_sc[...] = jnp.full_like(m_sc, -jnp.inf)
        l_sc[...] = jnp.zeros_like(l_sc)
        acc_sc[...] = jnp.zeros_like(acc_sc)

    lane = lax.broadcasted_iota(I32, (tq, LANES), 1)
    low = lane < HEAD_DIM

    def step(masked):
        ones = jnp.ones((tk, LANES), BF16)
        if masked:
            causal = (lax.broadcasted_iota(I32, (tq, tk), 0) >= lax.broadcasted_iota(I32, (tq, tk), 1))
        for j in range(ATTN_WIDTH // LANES):
            sl = slice(j * LANES, (j + 1) * LANES)
            qp = q_ref[0, :, sl]
            kp = k_ref[0, :, sl]
            vx = jnp.concatenate([v_ref[0, :, sl], ones], axis=1)
            zero = jnp.zeros_like(qp)
            alphas = []
            pvs = []
            for t in range(2):
                hd = 2 * j + t
                qh = jnp.where(low, qp, zero) if t == 0 else jnp.where(low, zero, qp)
                decay = (fq_ref[0, hd:hd + 1, 0:1] - fk_ref[0, hd:hd + 1, :]) * LOG2E
                s = _dot_nt(qh, kp) + decay
                if masked:
                    s = jnp.where(causal, s, -jnp.inf)
                m_prev = m_sc[hd]
                m_new = jnp.maximum(m_prev, jnp.max(s, axis=-1, keepdims=True))
                alpha = jnp.exp2(m_prev - m_new)
                p = jnp.concatenate([jnp.exp2(s[:, c * LANES:(c + 1) * LANES] - m_new)
                                     for c in range(tk // LANES)], axis=1)
                pv = _dot(p.astype(BF16), vx)
                l_sc[hd] = alpha * l_sc[hd] + pv[:, LANES:]
                m_sc[hd] = m_new
                alphas.append(alpha)
                pvs.append(pv[:, :LANES])
            acc_sc[j] = (acc_sc[j] * jnp.where(low, alphas[0], alphas[1])
                         + jnp.where(low, pvs[0], pvs[1]))

    @pl.when(ki < qi)
    def _():
        step(False)

    @pl.when(ki == qi)
    def _():
        step(True)
        for j in range(ATTN_WIDTH // LANES):
            inv = jnp.where(low, 1.0 / l_sc[2 * j], 1.0 / l_sc[2 * j + 1])
            o_ref[0, :, j * LANES:(j + 1) * LANES] = (acc_sc[j] * inv).astype(o_ref.dtype)


def _attn_prompt(q, k, v, fr, tq):
    b, s, w = q.shape
    nq = s // tq
    pairs = [(qi, ki) for qi in range(nq) for ki in range(qi + 1)]
    qi_arr = jnp.asarray([p[0] for p in pairs], I32)
    ki_arr = jnp.asarray([p[1] for p in pairs], I32)
    qmap = lambda bi, p, qa, ka: (bi, qa[p], 0)
    kmap = lambda bi, p, qa, ka: (bi, ka[p], 0)
    grid_spec = pltpu.PrefetchScalarGridSpec(
        num_scalar_prefetch=2,
        grid=(b, len(pairs)),
        in_specs=[pl.BlockSpec((1, tq, w), qmap),
                  pl.BlockSpec((1, tq, w), kmap),
                  pl.BlockSpec((1, tq, w), kmap),
                  pl.BlockSpec((1, N_HEADS, tq), lambda bi, p, qa, ka: (bi, 0, qa[p])),
                  pl.BlockSpec((1, N_HEADS, tq), lambda bi, p, qa, ka: (bi, 0, ka[p]))],
        out_specs=pl.BlockSpec((1, tq, w), qmap),
        scratch_shapes=[pltpu.VMEM((N_HEADS, tq, LANES), F32), pltpu.VMEM((N_HEADS, tq, LANES), F32),
                        pltpu.VMEM((w // LANES, tq, LANES), F32)],
    )
    return pl.pallas_call(
        _attn_kernel,
        grid_spec=grid_spec,
        out_shape=jax.ShapeDtypeStruct((b, s, w), BF16),
        compiler_params=_params("arbitrary", "arbitrary"),
        name="attn_prompt",
    )(qi_arr, ki_arr, q, k, v, fr, fr)


def _attn_decode_kernel(pt_ref, qbd_ref, kn_ref, vn_ref, lfn_ref, u_ref, *refs):
    n = PAGES_PER_STEP
    lf_refs, k_refs, v_refs = refs[:n], refs[n:2 * n], refs[2 * n:3 * n]
    o_ref, m_sc, l_sc, acc_sc, carry_sc = refs[3 * n:]
    j = pl.program_id(1)
    nj = pl.num_programs(1)
    width = N_HEADS * HEAD_DIM

    @pl.when(j == 0)
    def _():
        m_sc[...] = jnp.full_like(m_sc, -jnp.inf)
        l_sc[...] = jnp.zeros_like(l_sc)
        acc_sc[...] = jnp.zeros_like(acc_sc)
        carry_sc[...] = jnp.zeros_like(carry_sc)

    qbd = qbd_ref[0]
    lfn = lfn_ref[0]
    m, l, acc, carry = m_sc[...], l_sc[...], acc_sc[...], carry_sc[...]
    for i in range(n):
        lfp = lf_refs[i][0, 0]
        decay = (_dot3_right(lfp, u_ref[...]) + carry + lfn) * LOG2E
        carry = carry + jnp.sum(lfp, axis=-1, keepdims=True)
        kp = k_refs[i][0, 0].reshape(width, PAGE).astype(BF16)
        vp = v_refs[i][0, 0].reshape(width, PAGE).astype(BF16)
        s = _dot(qbd, kp) + decay
        m_new = jnp.maximum(m, jnp.max(s, axis=-1, keepdims=True))
        alpha = jnp.exp2(m - m_new)
        p = jnp.exp2(s - m_new)
        l = alpha * l + jnp.sum(p, axis=-1, keepdims=True)
        acc = alpha * acc + _dot_nt(p.astype(BF16), vp)
        m = m_new
    m_sc[...] = m
    l_sc[...] = l
    acc_sc[...] = acc
    carry_sc[...] = carry

    @pl.when(j == nj - 1)
    def _():
        s_n = jnp.sum(qbd.astype(F32) * kn_ref[0], axis=-1, keepdims=True)
        m_new = jnp.maximum(m, s_n)
        alpha = jnp.exp2(m - m_new)
        p_n = jnp.exp2(s_n - m_new)
        out = (alpha * acc + p_n * vn_ref[0]) / (alpha * l + p_n)
        own = (lax.broadcasted_iota(I32, (N_HEADS, width), 1) // HEAD_DIM
               == lax.broadcasted_iota(I32, (N_HEADS, width), 0))
        o_ref[0] = jnp.sum(jnp.where(own, out, 0.0), axis=0, keepdims=True)


def _attn_decode(page_table, qbd, k_new, v_new, lf_new, cache_lf_t, cache_k_t, cache_v_t):
    nb, n_pages = page_table.shape
    n = PAGES_PER_STEP
    width = N_HEADS * HEAD_DIM
    tok = lambda b, j, pt: (b, 0, 0)

    def page(i, rank):
        def index_map(b, j, pt):
            return (0, pt[b * n_pages + n_pages - 1 - (j * n + i)]) + (0,) * (rank - 2)
        return index_map

    lane = jnp.arange(PAGE)
    later = (lane[:, None] > lane[None, :]).astype(BF16)
    grid_spec = pltpu.PrefetchScalarGridSpec(
        num_scalar_prefetch=1,
        grid=(nb, n_pages // n),
        in_specs=[pl.BlockSpec((1, N_HEADS, width), tok),
                  pl.BlockSpec((1, 1, width), tok),
                  pl.BlockSpec((1, 1, width), tok),
                  pl.BlockSpec((1, N_HEADS, 1), tok),
                  pl.BlockSpec((PAGE, PAGE), lambda b, j, pt: (0, 0))]
                 + [pl.BlockSpec((1, 1, N_HEADS, PAGE), page(i, 4)) for i in range(n)]
                 + [pl.BlockSpec((1, 1, N_HEADS, HEAD_DIM, PAGE), page(i, 5)) for i in range(n)]
                 + [pl.BlockSpec((1, 1, N_HEADS, HEAD_DIM, PAGE), page(i, 5)) for i in range(n)],
        out_specs=pl.BlockSpec((1, 1, width), tok),
        scratch_shapes=[pltpu.VMEM((N_HEADS, 1), F32), pltpu.VMEM((N_HEADS, 1), F32),
                        pltpu.VMEM((N_HEADS, width), F32), pltpu.VMEM((N_HEADS, 1), F32)],
    )
    return pl.pallas_call(
        _attn_decode_kernel,
        grid_spec=grid_spec,
        out_shape=jax.ShapeDtypeStruct((nb, 1, width), F32),
        compiler_params=_params("arbitrary", "arbitrary"),
        name="attn_decode",
    )(page_table.reshape(-1), qbd, k_new, v_new, lf_new, later,
      *([cache_lf_t] * n), *([cache_k_t] * n), *([cache_v_t] * n))


def _mix_out_kernel(x_ref, oa_ref, sga_ref, mb_ref, mod_ref, wba_ref, wo_ref, gffn_ref,
                    wrh_ref, wrl_ref, wsg_ref, wsu_ref, wsd_ref,
                    h2_ref, lg_ref, base_ref):
    d = D_MODEL
    x = x_ref[0]
    gate1 = mod_ref[0, :, 2 * d:3 * d]
    shift2 = mod_ref[0, :, 3 * d:4 * d]
    scale2 = mod_ref[0, :, 4 * d:5 * d]
    gate2 = mod_ref[0, :, 5 * d:6 * d]
    merged = sga_ref[0].astype(F32) * _dot(oa_ref[0], wba_ref[...]) + mb_ref[0].astype(F32)
    x1 = x + gate1 * _dot(merged.astype(BF16), wo_ref[...])
    ms = jnp.mean(x1 * x1, axis=-1, keepdims=True)
    h2 = x1 * lax.rsqrt(ms + NORM_EPS) * gffn_ref[...] * (1.0 + scale2) + shift2
    h2_ref[0] = h2
    hb = h2.astype(BF16)
    hl = (h2 - hb.astype(F32)).astype(BF16)
    lg_ref[0] = _dot(hb, wrh_ref[...]) + (_dot(hb, wrl_ref[...]) + _dot(hl, wrh_ref[...]))
    g = _dot(hb, wsg_ref[...])
    u = _dot(hb, wsu_ref[...])
    a = (g * _sigmoid(g) * u).astype(BF16)
    base_ref[0] = x1 + gate2 * _dot(a, wsd_ref[...])


def _mix_out(x, oa, sga, mb, mod, wts, tl):
    bx, l, d = x.shape
    tlm = tl if mod.shape[1] > 1 else 1
    row = lambda b, i: (b, i, 0)
    mod_map = row if tlm > 1 else (lambda b, i: (b, 0, 0))
    names = ("wba", "wo", "gffn", "wrh", "wrl", "wsg", "wsu", "wsd")
    consts = [wts[n] for n in names]
    return pl.pallas_call(
        _mix_out_kernel,
        grid=(bx, l // tl),
        in_specs=[pl.BlockSpec((1, tl, d), row), pl.BlockSpec((1, tl, ATTN_WIDTH), row),
                  pl.BlockSpec((1, tl, d), row), pl.BlockSpec((1, tl, d), row),
                  pl.BlockSpec((1, tlm, 6 * d), mod_map)] + [_const_spec(c.shape) for c in consts],
        out_specs=[pl.BlockSpec((1, tl, d), row), pl.BlockSpec((1, tl, LANES), row),
                   pl.BlockSpec((1, tl, d), row)],
        out_shape=[jax.ShapeDtypeStruct((bx, l, d), F32), jax.ShapeDtypeStruct((bx, l, LANES), F32),
                   jax.ShapeDtypeStruct((bx, l, d), F32)],
        compiler_params=_params("arbitrary", "arbitrary"),
        name="mix_out",
    )(x, oa, sga, mb, mod, *consts)


def _route_kernel(lg_ref, b_ref, before_ref, idx_ref, w_ref, c_ref, cnt_ref):
    tt = lg_ref.shape[1]
    epg = EXPERTS_PER_GROUP
    ninf = -jnp.inf
    iota = lax.broadcasted_iota(I32, (epg, tt), 0)
    sc = []
    biased = []
    gscore = []
    for g in range(N_EXPERT_GROUPS):
        s = _sigmoid(lg_ref[g * epg:(g + 1) * epg, :])
        bz = s + b_ref[g * epg:(g + 1) * epg, :]
        m1 = jnp.max(bz, axis=0, keepdims=True)
        first = jnp.min(jnp.where(bz == m1, iota, epg), axis=0, keepdims=True)
        m2 = jnp.max(jnp.where(iota == first, ninf, bz), axis=0, keepdims=True)
        sc.append(s)
        biased.append(bz)
        gscore.append(m1 + m2)
    cand = []
    for g in range(N_EXPERT_GROUPS):
        rank = jnp.zeros((1, tt), I32)
        for o in range(N_EXPERT_GROUPS):
            if o == g:
                continue
            beats = (gscore[o] >= gscore[g]) if o < g else (gscore[o] > gscore[g])
            rank = rank + beats.astype(I32)
        cand.append(jnp.where(rank < TOP_K_GROUPS, biased[g], ninf))
    ws = []
    picks = []
    for k in range(TOP_K):
        mx = cand[0]
        for g in range(1, N_EXPERT_GROUPS):
            mx = jnp.maximum(mx, cand[g])
        mx = jnp.max(mx, axis=0, keepdims=True)
        fi = jnp.where(cand[0] == mx, iota, N_EXPERTS)
        for g in range(1, N_EXPERT_GROUPS):
            fi = jnp.minimum(fi, jnp.where(cand[g] == mx, iota + g * epg, N_EXPERTS))
        fi = jnp.min(fi, axis=0, keepdims=True)
        wk = jnp.zeros((epg, tt), F32)
        for g in range(N_EXPERT_GROUPS):
            hit = (iota + g * epg) == fi
            wk = wk + jnp.where(hit, sc[g], 0.0)
            cand[g] = jnp.where(hit, ninf, cand[g])
        idx_ref[k:k + 1, :] = fi
        picks.append(fi)
        ws.append(jnp.sum(wk, axis=0, keepdims=True))
    tot = ws[0]
    for k in range(1, TOP_K):
        tot = tot + ws[k]
    for k in range(TOP_K):
        w_ref[k:k + 1, :] = ws[k] / tot * ROUTED_SCALE

    @pl.when(pl.program_id(0) == 0)
    def _():
        cnt_ref[...] = jnp.zeros_like(cnt_ref)

    chosen = []
    for g in range(N_EXPERT_GROUPS):
        sel = jnp.zeros((epg, tt), F32)
        for k in range(TOP_K):
            sel = sel + jnp.where((iota + g * epg) == picks[k], 1.0, 0.0)
        chosen.append(sel)
    chosen = jnp.concatenate(chosen, axis=0)
    earlier = _dot(chosen.astype(BF16), before_ref[...]) + cnt_ref[...]
    for k in range(TOP_K):
        ck = jnp.zeros((epg, tt), F32)
        for g in range(N_EXPERT_GROUPS):
            ck = ck + jnp.where((iota + g * epg) == picks[k], earlier[g * epg:(g + 1) * epg, :], 0.0)
        c_ref[k:k + 1, :] = jnp.sum(ck, axis=0, keepdims=True).astype(I32)
    cnt_ref[...] += jnp.sum(chosen, axis=1, keepdims=True)


def _route(logits_t, b_router, tt):
    t = logits_t.shape[1]
    r = jnp.arange(tt)
    before = (r[:, None] < r[None, :]).astype(BF16)
    col = lambda i: (0, i)
    return pl.pallas_call(
        _route_kernel,
        grid=(t // tt,),
        in_specs=[pl.BlockSpec((N_EXPERTS, tt), col),
                  pl.BlockSpec((N_EXPERTS, 1), lambda i: (0, 0)),
                  pl.BlockSpec((tt, tt), lambda i: (0, 0))],
        out_specs=[pl.BlockSpec((TOP_K, tt), col), pl.BlockSpec((TOP_K, tt), col),
                   pl.BlockSpec((TOP_K, tt), col), pl.BlockSpec((N_EXPERTS, 1), lambda i: (0, 0))],
        out_shape=[jax.ShapeDtypeStruct((TOP_K, t), I32), jax.ShapeDtypeStruct((TOP_K, t), F32),
                   jax.ShapeDtypeStruct((TOP_K, t), I32), jax.ShapeDtypeStruct((N_EXPERTS, 1), F32)],
        compiler_params=_params("arbitrary"),
        name="route",
    )(logits_t, b_router.reshape(N_EXPERTS, 1), before)


def _dispatch_kernel(pos_ref, h_ref, xs_hbm, sem):
    tt = h_ref.shape[0]

    def copy(j, k):
        return pltpu.make_async_copy(h_ref.at[pl.ds(j, 1)],
                                     xs_hbm.at[pl.ds(pos_ref[0, 0, j * TOP_K + k], 1)], sem.at[0])

    def issue(j, c):
        for k in range(TOP_K):
            copy(j, k).start()
        return c

    def wait(j, c):
        for k in range(TOP_K):
            copy(j, k).wait()
        return c

    lax.fori_loop(0, tt, issue, 0, unroll=2)
    lax.fori_loop(0, tt, wait, 0, unroll=2)


def _dispatch(pos, h2, tt):
    t, d = h2.shape
    n = t // tt
    return pl.pallas_call(
        _dispatch_kernel,
        grid=(n,),
        in_specs=[pl.BlockSpec((1, 1, tt * TOP_K), lambda i: (i, 0, 0), memory_space=pltpu.SMEM),
                  pl.BlockSpec((tt, d), lambda i: (i, 0))],
        out_specs=pl.BlockSpec(memory_space=pl.ANY),
        out_shape=jax.ShapeDtypeStruct((t * TOP_K, d), F32),
        scratch_shapes=[pltpu.SemaphoreType.DMA((1,))],
        compiler_params=_params("arbitrary", row_dma=True),
        name="moe_dispatch",
    )(pos.reshape(n, 1, tt * TOP_K), h2)


VISIT_ACTIVE = 1
VISIT_FIRST = 2


def _moe_kernel(vt_ref, ve_ref, vf_ref, gs_ref, ge_ref, x_ref, wg_ref, wu_ref, wd_ref, o_ref):
    v = pl.program_id(0)
    tm = x_ref.shape[0]
    flags = vf_ref[v]

    @pl.when(flags >= VISIT_ACTIVE)
    def _():
        e = ve_ref[v]
        x = x_ref[...].astype(BF16)
        g = _dot(x, wg_ref[0].astype(BF16))
        u = _dot(x, wu_ref[0].astype(BF16))
        a = (g * _sigmoid(g) * u).astype(BF16)
        res = _dot(a, wd_ref[0].astype(BF16))
        row = vt_ref[v] * tm + lax.broadcasted_iota(I32, (tm, 1), 0)
        mine = (row >= gs_ref[e]) & (row < ge_ref[e])

        @pl.when(flags >= VISIT_FIRST)
        def _():
            o_ref[...] = jnp.where(mine, res, 0.0)

        @pl.when(flags < VISIT_FIRST)
        def _():
            o_ref[...] = jnp.where(mine, res, o_ref[...])


def _moe(plan, xs, w_gate, w_up, w_down, tm):
    visit_tile, visit_e, visit_flags, gs, ge = plan
    n_visits = visit_tile.shape[0]
    rows, d = xs.shape
    tile = lambda v, vt, ve, vf, s, e: (vt[v], 0)
    expert = lambda v, vt, ve, vf, s, e: (ve[v], 0, 0)
    grid_spec = pltpu.PrefetchScalarGridSpec(
        num_scalar_prefetch=5,
        grid=(n_visits,),
        in_specs=[pl.BlockSpec((tm, d), tile),
                  pl.BlockSpec((1, d, D_EXPERT), expert),
                  pl.BlockSpec((1, d, D_EXPERT), expert),
                  pl.BlockSpec((1, D_EXPERT, d), expert)],
        out_specs=pl.BlockSpec((tm, d), tile),
    )
    return pl.pallas_call(
        _moe_kernel,
        grid_spec=grid_spec,
        out_shape=jax.ShapeDtypeStruct((rows, d), F32),
        compiler_params=_params("arbitrary"),
        name="moe_experts",
    )(visit_tile, visit_e, visit_flags, gs, ge, xs, w_gate, w_up, w_down)


def _combine_kernel(pos_ref, posn_ref, eo_hbm, w_ref, base_ref, mod_ref, y_ref, buf, sem):
    tt = base_ref.shape[1]
    i = pl.program_id(0) * pl.num_programs(1) + pl.program_id(1)
    n = pl.num_programs(0) * pl.num_programs(1)
    slot = i % 2

    def copy(idx_ref, j, k, s):
        return pltpu.make_async_copy(eo_hbm.at[pl.ds(idx_ref[0, 0, j * TOP_K + k], 1)],
                                     buf.at[s, k, pl.ds(j, 1)], sem.at[s])

    def issue(idx_ref, s):
        def body(j, c):
            for k in range(TOP_K):
                copy(idx_ref, j, k, s).start()
            return c
        lax.fori_loop(0, tt, body, 0, unroll=2)

    @pl.when(i == 0)
    def _():
        issue(pos_ref, 0)

    @pl.when(i + 1 < n)
    def _():
        issue(posn_ref, 1 - slot)

    def wait(j, c):
        for k in range(TOP_K):
            copy(pos_ref, j, k, slot).wait()
        return c
    lax.fori_loop(0, tt, wait, 0, unroll=2)

    w = w_ref[0]
    routed = buf[slot, 0] * w[:, 0:1]
    for k in range(1, TOP_K):
        routed = routed + buf[slot, k] * w[:, k:k + 1]
    y_ref[0] = base_ref[0] + mod_ref[0, :, 5 * D_MODEL:6 * D_MODEL] * routed


def _combine(pos, eo, w, base, mod, tt):
    bx, l, d = base.shape
    nl = l // tt
    n = bx * nl
    pos3 = pos.reshape(n, 1, tt * TOP_K)
    tlm = tt if mod.shape[1] > 1 else 1
    row = lambda b, i: (b, i, 0)
    mod_map = row if tlm > 1 else (lambda b, i: (b, 0, 0))
    return pl.pallas_call(
        _combine_kernel,
        grid=(bx, nl),
        in_specs=[pl.BlockSpec((1, 1, tt * TOP_K), lambda b, i: (b * nl + i, 0, 0), memory_space=pltpu.SMEM),
                  pl.BlockSpec((1, 1, tt * TOP_K), lambda b, i: (jnp.minimum(b * nl + i + 1, n - 1), 0, 0),
                               memory_space=pltpu.SMEM),
                  pl.BlockSpec(memory_space=pl.ANY),
                  pl.BlockSpec((1, tt, TOP_K), row),
                  pl.BlockSpec((1, tt, d), row),
                  pl.BlockSpec((1, tlm, 6 * d), mod_map)],
        out_specs=pl.BlockSpec((1, tt, d), row),
        out_shape=jax.ShapeDtypeStruct((bx, l, d), F32),
        scratch_shapes=[pltpu.VMEM((2, TOP_K, tt, d), F32), pltpu.SemaphoreType.DMA((2,))],
        compiler_params=_params("arbitrary", "arbitrary", row_dma=True),
        name="moe_combine",
    )(pos3, pos3, eo, w, base, mod)


def _dispatch_plan(idx_t, c_t, counts, tm):
    k, t = idx_t.shape
    n_tiles = t * k // tm
    n_visits = n_tiles + N_EXPERTS
    counts = counts.reshape(N_EXPERTS).astype(I32)
    ge = jnp.cumsum(counts)
    gs = ge - counts
    experts = jnp.arange(N_EXPERTS, dtype=I32)
    base = jnp.sum(jnp.where(idx_t[None] == experts[:, None, None], gs[:, None, None], 0), axis=0)
    pos = (base + c_t).T.reshape(t * k)
    first_tile = gs // tm
    n_vis = jnp.where(counts > 0, (ge - 1) // tm - first_tile + 1, 0)
    vend = jnp.cumsum(n_vis)
    vstart = vend - n_vis
    v = jnp.arange(n_visits, dtype=I32)
    active = v < vend[-1]
    e_of_v = jnp.minimum(jnp.sum(vend[None, :] <= v[:, None], axis=1), N_EXPERTS - 1).astype(I32)
    onehot = e_of_v[:, None] == experts[None, :]
    pick = lambda table: jnp.sum(jnp.where(onehot, table[None, :], 0), axis=1)
    tile_of_v = jnp.where(active, pick(first_tile) + v - pick(vstart), n_tiles - 1).astype(I32)
    prev_tile = jnp.concatenate([jnp.full((1,), -1, I32), tile_of_v[:-1]])
    flags = jnp.where(active, VISIT_ACTIVE + VISIT_FIRST * (tile_of_v != prev_tile), 0).astype(I32)
    return pos, (tile_of_v, e_of_v, flags, gs, ge)


def _prep_weights(w_in, b_forget, g_q, g_k, g_vnorm, b_vnorm, g_norm_mix, g_norm_ffn,
                  w_branch_a, w_branch_b, w_out, w_router, w_sh_gate, w_sh_up, w_sh_down):
    aw, sw, d = ATTN_WIDTH, SGU_WIDTH, D_MODEL
    o = 3 * aw + N_HEADS
    wf = jnp.zeros((d, LANES), F32).at[:, :N_HEADS].set(w_in[:, 3 * aw:o])
    bf = jnp.zeros((1, LANES), F32).at[0, :N_HEADS].set(b_forget)
    lane = jnp.arange(aw)
    bd = (lane[:, None] // HEAD_DIM == lane[None, :] // HEAD_DIM).astype(BF16)
    wr = jnp.zeros((d, LANES), F32).at[:, :N_EXPERTS].set(w_router)
    wrh = wr.astype(BF16)
    wrl = (wr - wrh.astype(F32)).astype(BF16)
    return dict(
        gmix=g_norm_mix.reshape(1, d), gffn=g_norm_ffn.reshape(1, d),
        wq=w_in[:, 0:aw].astype(BF16), wk=w_in[:, aw:2 * aw].astype(BF16),
        wv=w_in[:, 2 * aw:3 * aw].astype(BF16), wf=wf.astype(BF16), bf=bf,
        wu=w_in[:, o:o + sw].astype(BF16), wvg=w_in[:, o + sw:o + 2 * sw].astype(BF16),
        wga=w_in[:, o + 2 * sw:o + 2 * sw + d].astype(BF16),
        wgb=w_in[:, o + 2 * sw + d:o + 2 * sw + 2 * d].astype(BF16),
        gq=jnp.tile(g_q, N_HEADS).reshape(1, aw), gk=jnp.tile(g_k, N_HEADS).reshape(1, aw),
        gvn=g_vnorm.reshape(1, sw), bvn=b_vnorm.reshape(1, sw), bd=bd,
        wbb=w_branch_b.astype(BF16), wba=w_branch_a.astype(BF16), wo=w_out.astype(BF16),
        wrh=wrh, wrl=wrl, wsg=w_sh_gate.astype(BF16), wsu=w_sh_up.astype(BF16),
        wsd=w_sh_down.astype(BF16))


def _spatial_weights(w_spatial, b_spatial, rows_are_sequences, tl):
    if rows_are_sequences:
        wsp = w_spatial[:, 0, 0][:, None, None] * jnp.eye(CHUNK, dtype=F32)[None]
        b = jnp.broadcast_to(b_spatial[:, 0:1], (SGU_GROUPS, CHUNK))
    else:
        wsp = jnp.where(jnp.tril(jnp.ones((CHUNK, CHUNK), bool)), w_spatial, 0)
        b = b_spatial
    half = LANES // 2
    bsp = jnp.repeat(b.reshape(SGU_GROUPS // 2, 2, CHUNK), half, axis=1)
    bsp = bsp.transpose(0, 2, 1)
    r = jnp.arange(tl)
    ltri = (r[:, None] >= r[None, :]).astype(BF16)
    return dict(wsp=wsp.astype(BF16), bsp=bsp, ltri=ltri)


def _layer(x, mod, attend, wts, w_spatial, b_spatial, b_router, w_exp_gate, w_exp_up, w_exp_down,
           rows_are_sequences, tl, tm, tt_route, tt_disp, tt_comb):
    bx, l, d = x.shape
    wts = dict(wts, **_spatial_weights(w_spatial, b_spatial, rows_are_sequences, tl))
    q, kf, vf, kb, vb, lf, fc, mb, sga, vn = _mix_in(x, mod, wts, tl)
    oa = attend(q, kf, vf, kb, vb, lf, fc)
    h2, logits, base = _mix_out(x, oa, sga, mb, mod, wts, tl)
    t = bx * l
    idx_t, w_t, c_t, counts = _route(logits.reshape(t, LANES).T, b_router, tt_route)
    pos, plan = _dispatch_plan(idx_t, c_t, counts, tm)
    xs = _dispatch(pos, h2.reshape(t, d), tt_disp)
    eo = _moe(plan, xs, w_exp_gate, w_exp_up, w_exp_down, tm)
    y = _combine(pos, eo, w_t.T.reshape(bx, l, TOP_K), base, mod, tt_comb)
    return y, kf, vf, lf, vn


def kernel(x_prompt, x_sample, c_prompt, c_sample, cache_k, cache_v, cache_logf, page_table, w_ada, b_ada, g_norm_mix, g_norm_ffn, w_in, b_forget, g_q, g_k, g_vnorm, b_vnorm, w_spatial, b_spatial, w_branch_a, w_branch_b, w_out, w_router, b_router, w_exp_gate, w_exp_up, w_exp_down, w_sh_gate, w_sh_up, w_sh_down):
    assert w_ada.shape[0] == 1, "one layer"
    b, s, d = x_prompt.shape
    nb = x_sample.shape[0]

    c_all = jnp.concatenate([c_prompt, c_sample], axis=0)
    pad = (-c_all.shape[0]) % 8
    c_all = jnp.pad(c_all, ((0, pad), (0, 0)))
    mod_all = _ada(c_all, w_ada[0], b_ada[0])
    mod_p = mod_all[:b].reshape(b, 1, 6 * d)
    mod_s = mod_all[b:b + nb].reshape(1, nb, 6 * d)

    wts = _prep_weights(w_in[0], b_forget[0], g_q[0], g_k[0], g_vnorm[0], b_vnorm[0], g_norm_mix[0],
                        g_norm_ffn[0], w_branch_a[0], w_branch_b[0], w_out[0], w_router[0],
                        w_sh_gate[0], w_sh_up[0], w_sh_down[0])
    experts = (w_exp_gate[0], w_exp_up[0], w_exp_down[0])

    tq = min(512, s)

    def attend_prompt(q, kf, vf, kb, vb, lf, fc):
        return _attn_prompt(q, kb, vb, fc.transpose(0, 2, 1), tq)

    def attend_sample(q, kf, vf, kb, vb, lf, fc):
        col = jnp.arange(ATTN_WIDTH) // HEAD_DIM
        own = col[None, None, :] == jnp.arange(N_HEADS)[None, :, None]
        qbd = jnp.where(own, q.reshape(nb, 1, ATTN_WIDTH), jnp.zeros((), BF16))
        o = _attn_decode(page_table, qbd, kf.reshape(nb, 1, ATTN_WIDTH), vf.reshape(nb, 1, ATTN_WIDTH),
                         lf.reshape(nb, N_HEADS, 1),
                         cache_logf.transpose(0, 1, 3, 2),
                         cache_k.transpose(0, 1, 3, 4, 2),
                         cache_v.transpose(0, 1, 3, 4, 2))
        return o.reshape(1, nb, ATTN_WIDTH).astype(BF16)

    y_s, k_s, v_s, lf_s, vn_s = _layer(x_sample.reshape(1, nb, d), mod_s, attend_sample, wts,
                                       w_spatial[0], b_spatial[0], b_router[0], *experts,
                                       rows_are_sequences=True, tl=nb, tm=32, tt_route=nb,
                                       tt_disp=nb, tt_comb=64)
    y_p, k_p, v_p, lf_p, _ = _layer(x_prompt, mod_p, attend_prompt, wts, w_spatial[0], b_spatial[0],
                                    b_router[0], *experts, rows_are_sequences=False,
                                    tl=min(512, s), tm=min(512, s), tt_route=min(1024, b * s),
                                    tt_disp=256, tt_comb=128)
    hd5 = (1, b, s, N_HEADS, HEAD_DIM)
    sd5 = (1, nb, 1, N_HEADS, HEAD_DIM)
    return (y_p, y_s.reshape(nb, 1, d),
            k_p.reshape(hd5), v_p.reshape(hd5), lf_p.reshape(1, b, s, N_HEADS),
            k_s.reshape(sd5), v_s.reshape(sd5), lf_s.reshape(1, nb, 1, N_HEADS),
            vn_s.reshape(1, nb, 1, SGU_WIDTH))
```

```python
import functools

import jax
import jax.numpy as jnp
from jax import lax
from jax.experimental import pallas as pl
from jax.experimental.pallas import tpu as pltpu

F32 = jnp.float32
BF16 = jnp.bfloat16
I32 = jnp.int32
U32 = jnp.uint32

D_MODEL = 1024
N_HEADS = 8
HEAD_DIM = 64
ATTN_WIDTH = N_HEADS * HEAD_DIM
SGU_GROUPS = 8
SGU_WIDTH = 512
CHUNK = 128
N_EXPERTS = 64
TOP_K = 8
N_EXPERT_GROUPS = 8
TOP_K_GROUPS = 4
EXPERTS_PER_GROUP = 8
D_EXPERT = 256
D_SHARED = 256
ROUTED_SCALE = 2.5
NORM_EPS = 1e-6
ATTN_SCALE = HEAD_DIM ** -0.5
LOG2E = 1.4426950408889634
PAGE = 128
LANES = 128
VMEM_LIMIT = 56 * 1024 * 1024
PAGES_PER_STEP = 8

_dot = functools.partial(jnp.dot, preferred_element_type=F32)


def _dot_nt(a, b):
    return lax.dot_general(a, b, (((1,), (1,)), ((), ())), preferred_element_type=F32)


def _sigmoid(x):
    return 1.0 / (1.0 + jnp.exp(-x))


def _gelu(x):
    return 0.5 * x * (1.0 + jnp.tanh(0.7978845608028654 * (x + 0.044715 * (x * x * x))))


def _split3(x):
    hi = x.astype(BF16)
    r1 = x - hi.astype(F32)
    mid = r1.astype(BF16)
    lo = (r1 - mid.astype(F32)).astype(BF16)
    return hi, mid, lo


def _pack_halves(x):
    w = x.shape[1] // 2
    lo = lax.bitcast_convert_type(x[:, :w].astype(BF16).astype(F32), U32)
    hi = lax.bitcast_convert_type(x[:, w:].astype(BF16).astype(F32), U32)
    return (hi & jnp.uint32(0xFFFF0000)) | (lo >> 16)


def _unpack_halves(p):
    return (lax.bitcast_convert_type(p << 16, F32),
            lax.bitcast_convert_type(p & jnp.uint32(0xFFFF0000), F32))


def _dot3_left(m_bf16, x):
    hi, mid, lo = _split3(x)
    return _dot(m_bf16, hi) + _dot(m_bf16, mid) + _dot(m_bf16, lo)


def _dot3_right(x, m_bf16):
    hi, mid, lo = _split3(x)
    return _dot(hi, m_bf16) + _dot(mid, m_bf16) + _dot(lo, m_bf16)


def _params(*sem, row_dma=False):
    return pltpu.CompilerParams(dimension_semantics=sem, vmem_limit_bytes=VMEM_LIMIT,
                                disable_bounds_checks=row_dma)


def _const_spec(shape):
    zeros = (0,) * len(shape)
    return pl.BlockSpec(shape, lambda *_: zeros)


def _ada_kernel(c_ref, w_ref, b_ref, o_ref):
    c = c_ref[...]
    a = c * _sigmoid(c)
    o_ref[...] = jnp.dot(a, w_ref[...], preferred_element_type=F32,
                         precision=lax.Precision.HIGHEST) + b_ref[...]


def _ada(c, w_ada, b_ada):
    m, d = c.shape
    n = w_ada.shape[1]
    tn = 1024
    return pl.pallas_call(
        _ada_kernel,
        grid=(n // tn,),
        in_specs=[pl.BlockSpec((m, d), lambda j: (0, 0)),
                  pl.BlockSpec((d, tn), lambda j: (0, j)),
                  pl.BlockSpec((1, tn), lambda j: (0, j))],
        out_specs=pl.BlockSpec((m, tn), lambda j: (0, j)),
        out_shape=jax.ShapeDtypeStruct((m, n), F32),
        compiler_params=_params("arbitrary"),
        name="ada",
    )(c, w_ada, b_ada.reshape(1, n))


def _mix_in_kernel(x_ref, mod_ref, gmix_ref, wq_ref, wk_ref, wv_ref, wf_ref, wu_ref, wvg_ref,
                   wga_ref, wgb_ref, bf_ref, gq_ref, gk_ref, gvn_ref, bvn_ref, bd_ref,
                   wsp_ref, bsp_ref, wbb_ref, ltri_ref,
                   q_ref, kf_ref, vf_ref, kb_ref, vb_ref, lf_ref, fc_ref, mb_ref, sga_ref, vn_ref,
                   carry_sc):
    tl = x_ref.shape[1]
    x = x_ref[0]
    shift1 = mod_ref[0, :, 0:D_MODEL]
    scale1 = mod_ref[0, :, D_MODEL:2 * D_MODEL]
    ms = jnp.mean(x * x, axis=-1, keepdims=True)
    h = x * lax.rsqrt(ms + NORM_EPS) * gmix_ref[...] * (1.0 + scale1) + shift1
    hb = h.astype(BF16)
    bd = bd_ref[...]

    def head_norm(z, g):
        ss = _dot((z * z).astype(BF16), bd) * (1.0 / HEAD_DIM)
        return z * lax.rsqrt(ss + NORM_EPS) * g

    qn = head_norm(_dot(hb, wq_ref[...]), gq_ref[...]) * (ATTN_SCALE * LOG2E)
    q_ref[0] = qn.astype(BF16)
    kn = head_norm(_dot(hb, wk_ref[...]), gk_ref[...])
    kf_ref[0] = kn
    kb_ref[0] = kn.astype(BF16)
    v = _dot(hb, wv_ref[...])
    vf_ref[0] = v
    vb_ref[0] = v.astype(BF16)

    zf = _dot(hb, wf_ref[...]) + bf_ref[...]
    lf = jnp.minimum(zf, 0.0) - jnp.log(1.0 + jnp.exp(-jnp.abs(zf)))
    lf_ref[0] = lf[:, :N_HEADS]

    @pl.when(pl.program_id(1) == 0)
    def _():
        carry_sc[...] = jnp.zeros_like(carry_sc)

    fc = _dot3_left(ltri_ref[...], lf) + carry_sc[...]
    fc_ref[0] = fc[:, :N_HEADS]
    carry_sc[...] = fc[tl - 1:tl, :]

    gu = _gelu(_dot(hb, wu_ref[...]))
    gv = _gelu(_dot(hb, wvg_ref[...]))
    mu = jnp.mean(gv, axis=-1, keepdims=True)
    gc = gv - mu
    var = jnp.mean(gc * gc, axis=-1, keepdims=True)
    vn = gc * lax.rsqrt(var + NORM_EPS) * gvn_ref[...] + bvn_ref[...]
    vn_ref[0] = vn
    vnb = vn.astype(BF16)
    lane = lax.broadcasted_iota(I32, (CHUNK, LANES), 1)
    low = lane < (LANES // 2)
    zero = jnp.zeros((CHUNK, LANES), BF16)
    rows = []
    for c in range(tl // CHUNK):
        pieces = []
        for j in range(SGU_WIDTH // LANES):
            vp = vnb[c * CHUNK:(c + 1) * CHUNK, j * LANES:(j + 1) * LANES]
            mixed = (_dot(wsp_ref[2 * j], jnp.where(low, vp, zero))
                     + _dot(wsp_ref[2 * j + 1], jnp.where(low, zero, vp)) + bsp_ref[j])
            pieces.append(mixed)
        rows.append(jnp.concatenate(pieces, axis=1))
    mixed = rows[0] if len(rows) == 1 else jnp.concatenate(rows, axis=0)
    ob = (gu * mixed).astype(BF16)
    mb = _sigmoid(_dot(hb, wgb_ref[...])) * _dot(ob, wbb_ref[...])
    mb_ref[0] = mb.astype(BF16)
    sga_ref[0] = _sigmoid(_dot(hb, wga_ref[...])).astype(BF16)


def _mix_in(x, mod, wts, tl):
    bx, l, d = x.shape
    tlm = tl if mod.shape[1] > 1 else 1
    grid = (bx, l // tl)
    row = lambda b, i: (b, i, 0)
    mod_map = row if tlm > 1 else (lambda b, i: (b, 0, 0))
    names = ("gmix", "wq", "wk", "wv", "wf", "wu", "wvg", "wga", "wgb", "bf", "gq", "gk", "gvn",
             "bvn", "bd", "wsp", "bsp", "wbb", "ltri")
    consts = [wts[n] for n in names]
    out_widths = [(ATTN_WIDTH, BF16), (ATTN_WIDTH, F32), (ATTN_WIDTH, F32), (ATTN_WIDTH, BF16),
                  (ATTN_WIDTH, BF16), (N_HEADS, F32), (N_HEADS, F32), (D_MODEL, BF16),
                  (D_MODEL, BF16), (SGU_WIDTH, F32)]
    return pl.pallas_call(
        _mix_in_kernel,
        grid=grid,
        in_specs=[pl.BlockSpec((1, tl, d), row), pl.BlockSpec((1, tlm, 6 * d), mod_map)]
                 + [_const_spec(c.shape) for c in consts],
        out_specs=[pl.BlockSpec((1, tl, w), row) for w, _ in out_widths],
        out_shape=[jax.ShapeDtypeStruct((bx, l, w), dt) for w, dt in out_widths],
        scratch_shapes=[pltpu.VMEM((1, LANES), F32)],
        compiler_params=_params("arbitrary", "arbitrary"),
        name="mix_in",
    )(x, mod, *consts)


def _attn_kernel(qi_ref, ki_ref, q_ref, k_ref, v_ref, fq_ref, fk_ref, o_ref, m_sc, l_sc, acc_sc):
    tq = q_ref.shape[1]
    tk = k_ref.shape[1]
    pair = pl.program_id(1)
    qi = qi_ref[pair]
    ki = ki_ref[pair]

    @pl.when(ki == 0)
    def _():
        m_sc[...] = jnp.full_like(m_sc, -jnp.inf)
        l_sc[...] = jnp.zeros_like(l_sc)
        acc_sc[...] = jnp.zeros_like(acc_sc)

    lane = lax.broadcasted_iota(I32, (tq, LANES), 1)
    low = lane < HEAD_DIM

    def step(masked):
        ones = jnp.ones((tk, LANES), BF16)
        if masked:
            causal = (lax.broadcasted_iota(I32, (tq, tk), 0) >= lax.broadcasted_iota(I32, (tq, tk), 1))
        for j in range(ATTN_WIDTH // LANES):
            sl = slice(j * LANES, (j + 1) * LANES)
            qp = q_ref[0, :, sl]
            kp = k_ref[0, :, sl]
            vx = jnp.concatenate([v_ref[0, :, sl], ones], axis=1)
            zero = jnp.zeros_like(qp)
            alphas = []
            pvs = []
            for t in range(2):
                hd = 2 * j + t
                qh = jnp.where(low, qp, zero) if t == 0 else jnp.where(low, zero, qp)
                decay = (fq_ref[0, hd:hd + 1, 0:1] - fk_ref[0, hd:hd + 1, :]) * LOG2E
                s = _dot_nt(qh, kp) + decay
                if masked:
                    s = jnp.where(causal, s, -jnp.inf)
                m_prev = m_sc[hd]
                m_new = jnp.maximum(m_prev, jnp.max(s, axis=-1, keepdims=True))
                alpha = jnp.exp2(m_prev - m_new)
                p = jnp.concatenate([jnp.exp2(s[:, c * LANES:(c + 1) * LANES] - m_new)
                                     for c in range(tk // LANES)], axis=1)
                pv = _dot(p.astype(BF16), vx)
                l_sc[hd] = alpha * l_sc[hd] + pv[:, LANES:]
                m_sc[hd] = m_new
                alphas.append(alpha)
                pvs.append(pv[:, :LANES])
            acc_sc[j] = (acc_sc[j] * jnp.where(low, alphas[0], alphas[1])
                         + jnp.where(low, pvs[0], pvs[1]))

    @pl.when(ki < qi)
    def _():
        step(False)

    @pl.when(ki == qi)
    def _():
        step(True)
        for j in range(ATTN_WIDTH // LANES):
            inv = jnp.where(low, 1.0 / l_sc[2 * j], 1.0 / l_sc[2 * j + 1])
            o_ref[0, :, j * LANES:(j + 1) * LANES] = (acc_sc[j] * inv).astype(o_ref.dtype)


def _attn_prompt(q, k, v, fr, tq):
    b, s, w = q.shape
    nq = s // tq
    pairs = [(qi, ki) for qi in range(nq) for ki in range(qi + 1)]
    qi_arr = jnp.asarray([p[0] for p in pairs], I32)
    ki_arr = jnp.asarray([p[1] for p in pairs], I32)
    qmap = lambda bi, p, qa, ka: (bi, qa[p], 0)
    kmap = lambda bi, p, qa, ka: (bi, ka[p], 0)
    grid_spec = pltpu.PrefetchScalarGridSpec(
        num_scalar_prefetch=2,
        grid=(b, len(pairs)),
        in_specs=[pl.BlockSpec((1, tq, w), qmap),
                  pl.BlockSpec((1, tq, w), kmap),
                  pl.BlockSpec((1, tq, w), kmap),
                  pl.BlockSpec((1, N_HEADS, tq), lambda bi, p, qa, ka: (bi, 0, qa[p])),
                  pl.BlockSpec((1, N_HEADS, tq), lambda bi, p, qa, ka: (bi, 0, ka[p]))],
        out_specs=pl.BlockSpec((1, tq, w), qmap),
        scratch_shapes=[pltpu.VMEM((N_HEADS, tq, LANES), F32), pltpu.VMEM((N_HEADS, tq, LANES), F32),
                        pltpu.VMEM((w // LANES, tq, LANES), F32)],
    )
    return pl.pallas_call(
        _attn_kernel,
        grid_spec=grid_spec,
        out_shape=jax.ShapeDtypeStruct((b, s, w), BF16),
        compiler_params=_params("arbitrary", "arbitrary"),
        name="attn_prompt",
    )(qi_arr, ki_arr, q, k, v, fr, fr)


def _rows_to_tile(rows):
    sub = lax.broadcasted_iota(I32, (N_HEADS, PAGE), 0)
    tile = jnp.zeros((N_HEADS, PAGE), F32)
    for h, r in enumerate(rows):
        tile = jnp.where(sub == h, r, tile)
    return tile


def _attn_decode_kernel(pt_ref, q_ref, kn_ref, vn_ref, lfn_ref, u_ref, *refs):
    n = PAGES_PER_STEP
    lf_refs, k_refs, v_refs = refs[:n], refs[n:2 * n], refs[2 * n:3 * n]
    o_ref, m_sc, l_sc, acc_sc, carry_sc = refs[3 * n:]
    j = pl.program_id(1)
    nj = pl.num_programs(1)
    heads = range(N_HEADS)

    @pl.when(j == 0)
    def _():
        m_sc[...] = jnp.full_like(m_sc, -jnp.inf)
        l_sc[...] = jnp.zeros_like(l_sc)
        acc_sc[...] = jnp.zeros_like(acc_sc)
        carry_sc[...] = jnp.zeros_like(carry_sc)

    lfn = lfn_ref[0]
    lf_all = jnp.concatenate([lf_refs[i][0, 0] for i in range(n)], axis=0)
    later_all = _dot3_right(lf_all, u_ref[...])
    carry = carry_sc[...]
    decays = []
    for i in range(n):
        sl = slice(i * N_HEADS, (i + 1) * N_HEADS)
        decays.append((later_all[sl] + carry + lfn) * LOG2E)
        carry = carry + jnp.sum(lf_all[sl], axis=-1, keepdims=True)
    carry_sc[...] = carry

    rows = [[None] * N_HEADS for _ in range(n)]
    for h in heads:
        qh = q_ref[0, h]
        for i in range(n):
            rows[i][h] = jnp.sum(k_refs[i][0, 0, h] * qh, axis=0, keepdims=True)
    s = [_rows_to_tile(rows[i]) + decays[i] for i in range(n)]
    m = m_sc[...]
    m_new = m
    for i in range(n):
        m_new = jnp.maximum(m_new, jnp.max(s[i], axis=-1, keepdims=True))
    alpha = jnp.exp2(m - m_new)
    p = [jnp.exp2(s[i] - m_new) for i in range(n)]
    l = alpha * l_sc[...]
    for i in range(n):
        l = l + jnp.sum(p[i], axis=-1, keepdims=True)
    m_sc[...] = m_new
    l_sc[...] = l
    alpha_rep = jnp.broadcast_to(alpha, (N_HEADS, PAGE))
    for h in heads:
        a = acc_sc[h] * alpha_rep[h:h + 1, :]
        for i in range(n):
            a = a + p[i][h:h + 1, :] * v_refs[i][0, 0, h]
        acc_sc[h] = a

    @pl.when(j == nj - 1)
    def _():
        s_n = _rows_to_tile([jnp.sum(q_ref[0, h] * kn_ref[0, h], axis=0, keepdims=True) for h in heads])
        m_rep = jnp.broadcast_to(m_new, (N_HEADS, PAGE))
        m_fin = jnp.maximum(m_rep, s_n)
        a_fin = jnp.exp2(m_rep - m_fin)
        p_n = jnp.exp2(s_n - m_fin)
        inv = 1.0 / (a_fin * jnp.broadcast_to(l, (N_HEADS, PAGE)) + p_n)
        for h in heads:
            total = jnp.broadcast_to(jnp.sum(acc_sc[h], axis=-1, keepdims=True), (HEAD_DIM, PAGE))
            o_ref[0, h] = (a_fin[h:h + 1, :] * total + p_n[h:h + 1, :] * vn_ref[0, h]) * inv[h:h + 1, :]


def _attn_decode(page_table, q_rep, k_new_rep, v_new_rep, lf_new, cache_lf_t, cache_k_t, cache_v_t):
    nb, n_pages = page_table.shape
    n = PAGES_PER_STEP
    tok3 = lambda b, j, pt: (b, 0, 0)
    tok4 = lambda b, j, pt: (b, 0, 0, 0)

    def page(i, rank):
        def index_map(b, j, pt):
            return (0, pt[b * n_pages + n_pages - 1 - (j * n + i)]) + (0,) * (rank - 2)
        return index_map

    lane = jnp.arange(PAGE)
    later = (lane[:, None] > lane[None, :]).astype(BF16)
    rep = pl.BlockSpec((1, N_HEADS, HEAD_DIM, PAGE), tok4)
    grid_spec = pltpu.PrefetchScalarGridSpec(
        num_scalar_prefetch=1,
        grid=(nb, n_pages // n),
        in_specs=[rep, rep, rep,
                  pl.BlockSpec((1, N_HEADS, 1), tok3),
                  pl.BlockSpec((PAGE, PAGE), lambda b, j, pt: (0, 0))]
                 + [pl.BlockSpec((1, 1, N_HEADS, PAGE), page(i, 4)) for i in range(n)]
                 + [pl.BlockSpec((1, 1, N_HEADS, HEAD_DIM, PAGE), page(i, 5)) for i in range(n)]
                 + [pl.BlockSpec((1, 1, N_HEADS, HEAD_DIM, PAGE), page(i, 5)) for i in range(n)],
        out_specs=rep,
        scratch_shapes=[pltpu.VMEM((N_HEADS, 1), F32), pltpu.VMEM((N_HEADS, 1), F32),
                        pltpu.VMEM((N_HEADS, HEAD_DIM, PAGE), F32), pltpu.VMEM((N_HEADS, 1), F32)],
    )
    return pl.pallas_call(
        _attn_decode_kernel,
        grid_spec=grid_spec,
        out_shape=jax.ShapeDtypeStruct((nb, N_HEADS, HEAD_DIM, PAGE), F32),
        compiler_params=_params("arbitrary", "arbitrary"),
        name="attn_decode",
    )(page_table.reshape(-1), q_rep, k_new_rep, v_new_rep, lf_new, later,
      *([cache_lf_t] * n), *([cache_k_t] * n), *([cache_v_t] * n))


def _mix_out_kernel(x_ref, oa_ref, sga_ref, mb_ref, mod_ref, wba_ref, wo_ref, gffn_ref,
                    wrh_ref, wrl_ref, wsg_ref, wsu_ref, wsd_ref,
                    h2_ref, lg_ref, base_ref):
    d = D_MODEL
    x = x_ref[0]
    gate1 = mod_ref[0, :, 2 * d:3 * d]
    shift2 = mod_ref[0, :, 3 * d:4 * d]
    scale2 = mod_ref[0, :, 4 * d:5 * d]
    gate2 = mod_ref[0, :, 5 * d:6 * d]
    merged = sga_ref[0].astype(F32) * _dot(oa_ref[0], wba_ref[...]) + mb_ref[0].astype(F32)
    x1 = x + gate1 * _dot(merged.astype(BF16), wo_ref[...])
    ms = jnp.mean(x1 * x1, axis=-1, keepdims=True)
    h2 = x1 * lax.rsqrt(ms + NORM_EPS) * gffn_ref[...] * (1.0 + scale2) + shift2
    h2_ref[0] = _pack_halves(h2)
    hb = h2.astype(BF16)
    hl = (h2 - hb.astype(F32)).astype(BF16)
    lg_ref[0] = _dot(hb, wrh_ref[...]) + (_dot(hb, wrl_ref[...]) + _dot(hl, wrh_ref[...]))
    g = _dot(hb, wsg_ref[...])
    u = _dot(hb, wsu_ref[...])
    a = (g * _sigmoid(g) * u).astype(BF16)
    base_ref[0] = x1 + gate2 * _dot(a, wsd_ref[...])


def _mix_out(x, oa, sga, mb, mod, wts, tl):
    bx, l, d = x.shape
    tlm = tl if mod.shape[1] > 1 else 1
    row = lambda b, i: (b, i, 0)
    mod_map = row if tlm > 1 else (lambda b, i: (b, 0, 0))
    names = ("wba", "wo", "gffn", "wrh", "wrl", "wsg", "wsu", "wsd")
    consts = [wts[n] for n in names]
    return pl.pallas_call(
        _mix_out_kernel,
        grid=(bx, l // tl),
        in_specs=[pl.BlockSpec((1, tl, d), row), pl.BlockSpec((1, tl, ATTN_WIDTH), row),
                  pl.BlockSpec((1, tl, d), row), pl.BlockSpec((1, tl, d), row),
                  pl.BlockSpec((1, tlm, 6 * d), mod_map)] + [_const_spec(c.shape) for c in consts],
        out_specs=[pl.BlockSpec((1, tl, d // 2), row), pl.BlockSpec((1, tl, LANES), row),
                   pl.BlockSpec((1, tl, d), row)],
        out_shape=[jax.ShapeDtypeStruct((bx, l, d // 2), U32), jax.ShapeDtypeStruct((bx, l, LANES), F32),
                   jax.ShapeDtypeStruct((bx, l, d), F32)],
        compiler_params=_params("arbitrary", "arbitrary"),
        name="mix_out",
    )(x, oa, sga, mb, mod, *consts)


def _route_kernel(lg_ref, b_ref, before_ref, idx_ref, w_ref, c_ref, cnt_ref):
    tt = lg_ref.shape[1]
    epg = EXPERTS_PER_GROUP
    ninf = -jnp.inf
    iota = lax.broadcasted_iota(I32, (epg, tt), 0)
    sc = []
    biased = []
    gscore = []
    for g in range(N_EXPERT_GROUPS):
        s = _sigmoid(lg_ref[g * epg:(g + 1) * epg, :])
        bz = s + b_ref[g * epg:(g + 1) * epg, :]
        m1 = jnp.max(bz, axis=0, keepdims=True)
        first = jnp.min(jnp.where(bz == m1, iota, epg), axis=0, keepdims=True)
        m2 = jnp.max(jnp.where(iota == first, ninf, bz), axis=0, keepdims=True)
        sc.append(s)
        biased.append(bz)
        gscore.append(m1 + m2)
    cand = []
    for g in range(N_EXPERT_GROUPS):
        rank = jnp.zeros((1, tt), I32)
        for o in range(N_EXPERT_GROUPS):
            if o == g:
                continue
            beats = (gscore[o] >= gscore[g]) if o < g else (gscore[o] > gscore[g])
            rank = rank + beats.astype(I32)
        cand.append(jnp.where(rank < TOP_K_GROUPS, biased[g], ninf))
    ws = []
    picks = []
    for k in range(TOP_K):
        mx = cand[0]
        for g in range(1, N_EXPERT_GROUPS):
            mx = jnp.maximum(mx, cand[g])
        mx = jnp.max(mx, axis=0, keepdims=True)
        fi = jnp.where(cand[0] == mx, iota, N_EXPERTS)
        for g in range(1, N_EXPERT_GROUPS):
            fi = jnp.minimum(fi, jnp.where(cand[g] == mx, iota + g * epg, N_EXPERTS))
        fi = jnp.min(fi, axis=0, keepdims=True)
        wk = jnp.zeros((epg, tt), F32)
        for g in range(N_EXPERT_GROUPS):
            hit = (iota + g * epg) == fi
            wk = wk + jnp.where(hit, sc[g], 0.0)
            cand[g] = jnp.where(hit, ninf, cand[g])
        idx_ref[k:k + 1, :] = fi
        picks.append(fi)
        ws.append(jnp.sum(wk, axis=0, keepdims=True))
    tot = ws[0]
    for k in range(1, TOP_K):
        tot = tot + ws[k]
    for k in range(TOP_K):
        w_ref[k:k + 1, :] = ws[k] / tot * ROUTED_SCALE

    @pl.when(pl.program_id(0) == 0)
    def _():
        cnt_ref[...] = jnp.zeros_like(cnt_ref)

    chosen = []
    for g in range(N_EXPERT_GROUPS):
        sel = jnp.zeros((epg, tt), F32)
        for k in range(TOP_K):
            sel = sel + jnp.where((iota + g * epg) == picks[k], 1.0, 0.0)
        chosen.append(sel)
    chosen = jnp.concatenate(chosen, axis=0)
    earlier = _dot(chosen.astype(BF16), before_ref[...]) + cnt_ref[...]
    for k in range(TOP_K):
        ck = jnp.zeros((epg, tt), F32)
        for g in range(N_EXPERT_GROUPS):
            ck = ck + jnp.where((iota + g * epg) == picks[k], earlier[g * epg:(g + 1) * epg, :], 0.0)
        c_ref[k:k + 1, :] = jnp.sum(ck, axis=0, keepdims=True).astype(I32)
    cnt_ref[...] += jnp.sum(chosen, axis=1, keepdims=True)


def _route(logits_t, b_router, tt):
    t = logits_t.shape[1]
    r = jnp.arange(tt)
    before = (r[:, None] < r[None, :]).astype(BF16)
    col = lambda i: (0, i)
    return pl.pallas_call(
        _route_kernel,
        grid=(t // tt,),
        in_specs=[pl.BlockSpec((N_EXPERTS, tt), col),
                  pl.BlockSpec((N_EXPERTS, 1), lambda i: (0, 0)),
                  pl.BlockSpec((tt, tt), lambda i: (0, 0))],
        out_specs=[pl.BlockSpec((TOP_K, tt), col), pl.BlockSpec((TOP_K, tt), col),
                   pl.BlockSpec((TOP_K, tt), col), pl.BlockSpec((N_EXPERTS, 1), lambda i: (0, 0))],
        out_shape=[jax.ShapeDtypeStruct((TOP_K, t), I32), jax.ShapeDtypeStruct((TOP_K, t), F32),
                   jax.ShapeDtypeStruct((TOP_K, t), I32), jax.ShapeDtypeStruct((N_EXPERTS, 1), F32)],
        compiler_params=_params("arbitrary"),
        name="route",
    )(logits_t, b_router.reshape(N_EXPERTS, 1), before)


def _dispatch_kernel(pos_ref, h_ref, xs_hbm, sem):
    tt = h_ref.shape[0]

    def copy(j, k):
        return pltpu.make_async_copy(h_ref.at[pl.ds(j, 1)],
                                     xs_hbm.at[pl.ds(pos_ref[0, 0, j * TOP_K + k], 1)], sem.at[0])

    def issue(j, c):
        for k in range(TOP_K):
            copy(j, k).start()
        return c

    def wait(j, c):
        for k in range(TOP_K):
            copy(j, k).wait()
        return c

    lax.fori_loop(0, tt, issue, 0, unroll=2)
    lax.fori_loop(0, tt, wait, 0, unroll=2)


def _dispatch(pos, h2, tt):
    t, d = h2.shape
    n = t // tt
    return pl.pallas_call(
        _dispatch_kernel,
        grid=(n,),
        in_specs=[pl.BlockSpec((1, 1, tt * TOP_K), lambda i: (i, 0, 0), memory_space=pltpu.SMEM),
                  pl.BlockSpec((tt, d), lambda i: (i, 0))],
        out_specs=pl.BlockSpec(memory_space=pl.ANY),
        out_shape=jax.ShapeDtypeStruct((t * TOP_K, d), h2.dtype),
        scratch_shapes=[pltpu.SemaphoreType.DMA((1,))],
        compiler_params=_params("arbitrary", row_dma=True),
        name="moe_dispatch",
    )(pos.reshape(n, 1, tt * TOP_K), h2)


VISIT_ACTIVE = 1
VISIT_FIRST = 2


def _moe_kernel(vt_ref, ve_ref, vf_ref, gs_ref, ge_ref, x_ref, wg_ref, wu_ref, wd_ref, o_ref):
    v = pl.program_id(0)
    tm = x_ref.shape[0]
    flags = vf_ref[v]

    @pl.when(flags >= VISIT_ACTIVE)
    def _():
        e = ve_ref[v]
        x = jnp.concatenate(_unpack_halves(x_ref[...]), axis=1).astype(BF16)
        g = _dot(x, wg_ref[0].astype(BF16))
        u = _dot(x, wu_ref[0].astype(BF16))
        a = (g * _sigmoid(g) * u).astype(BF16)
        res = _pack_halves(_dot(a, wd_ref[0].astype(BF16)))
        row = vt_ref[v] * tm + lax.broadcasted_iota(I32, (tm, 1), 0)
        mine = (row >= gs_ref[e]) & (row < ge_ref[e])

        @pl.when(flags >= VISIT_FIRST)
        def _():
            o_ref[...] = jnp.where(mine, res, jnp.zeros_like(res))

        @pl.when(flags < VISIT_FIRST)
        def _():
            o_ref[...] = jnp.where(mine, res, o_ref[...])


def _moe(plan, xs, w_gate, w_up, w_down, tm):
    visit_tile, visit_e, visit_flags, gs, ge = plan
    n_visits = visit_tile.shape[0]
    rows, dp = xs.shape
    d = 2 * dp
    tile = lambda v, vt, ve, vf, s, e: (vt[v], 0)
    expert = lambda v, vt, ve, vf, s, e: (ve[v], 0, 0)
    grid_spec = pltpu.PrefetchScalarGridSpec(
        num_scalar_prefetch=5,
        grid=(n_visits,),
        in_specs=[pl.BlockSpec((tm, dp), tile),
                  pl.BlockSpec((1, d, D_EXPERT), expert),
                  pl.BlockSpec((1, d, D_EXPERT), expert),
                  pl.BlockSpec((1, D_EXPERT, d), expert)],
        out_specs=pl.BlockSpec((tm, dp), tile),
    )
    return pl.pallas_call(
        _moe_kernel,
        grid_spec=grid_spec,
        out_shape=jax.ShapeDtypeStruct((rows, dp), U32),
        compiler_params=_params("arbitrary"),
        name="moe_experts",
    )(visit_tile, visit_e, visit_flags, gs, ge, xs, w_gate, w_up, w_down)


def _combine_kernel(pos_ref, posn_ref, eo_hbm, w_ref, base_ref, mod_ref, y_ref, buf, sem):
    tt = base_ref.shape[1]
    i = pl.program_id(0) * pl.num_programs(1) + pl.program_id(1)
    n = pl.num_programs(0) * pl.num_programs(1)
    slot = i % 2

    def copy(idx_ref, j, k, s):
        return pltpu.make_async_copy(eo_hbm.at[pl.ds(idx_ref[0, 0, j * TOP_K + k], 1)],
                                     buf.at[s, k, pl.ds(j, 1)], sem.at[s])

    def issue(idx_ref, s):
        def body(j, c):
            for k in range(TOP_K):
                copy(idx_ref, j, k, s).start()
            return c
        lax.fori_loop(0, tt, body, 0, unroll=2)

    @pl.when(i == 0)
    def _():
        issue(pos_ref, 0)

    @pl.when(i + 1 < n)
    def _():
        issue(posn_ref, 1 - slot)

    def wait(j, c):
        for k in range(TOP_K):
            copy(pos_ref, j, k, slot).wait()
        return c
    lax.fori_loop(0, tt, wait, 0, unroll=2)

    w = w_ref[0]
    half = D_MODEL // 2
    lo, hi = _unpack_halves(buf[slot, 0])
    lo, hi = lo * w[:, 0:1], hi * w[:, 0:1]
    for k in range(1, TOP_K):
        lo_k, hi_k = _unpack_halves(buf[slot, k])
        lo, hi = lo + lo_k * w[:, k:k + 1], hi + hi_k * w[:, k:k + 1]
    gate2 = mod_ref[0, :, 5 * D_MODEL:6 * D_MODEL]
    y_ref[0, :, :half] = base_ref[0, :, :half] + gate2[:, :half] * lo
    y_ref[0, :, half:] = base_ref[0, :, half:] + gate2[:, half:] * hi


def _combine(pos, eo, w, base, mod, tt):
    bx, l, d = base.shape
    nl = l // tt
    n = bx * nl
    pos3 = pos.reshape(n, 1, tt * TOP_K)
    tlm = tt if mod.shape[1] > 1 else 1
    row = lambda b, i: (b, i, 0)
    mod_map = row if tlm > 1 else (lambda b, i: (b, 0, 0))
    return pl.pallas_call(
        _combine_kernel,
        grid=(bx, nl),
        in_specs=[pl.BlockSpec((1, 1, tt * TOP_K), lambda b, i: (b * nl + i, 0, 0), memory_space=pltpu.SMEM),
                  pl.BlockSpec((1, 1, tt * TOP_K), lambda b, i: (jnp.minimum(b * nl + i + 1, n - 1), 0, 0),
                               memory_space=pltpu.SMEM),
                  pl.BlockSpec(memory_space=pl.ANY),
                  pl.BlockSpec((1, tt, TOP_K), row),
                  pl.BlockSpec((1, tt, d), row),
                  pl.BlockSpec((1, tlm, 6 * d), mod_map)],
        out_specs=pl.BlockSpec((1, tt, d), row),
        out_shape=jax.ShapeDtypeStruct((bx, l, d), F32),
        scratch_shapes=[pltpu.VMEM((2, TOP_K, tt, d // 2), U32), pltpu.SemaphoreType.DMA((2,))],
        compiler_params=_params("arbitrary", "arbitrary", row_dma=True),
        name="moe_combine",
    )(pos3, pos3, eo, w, base, mod)


def _dispatch_plan(idx_t, c_t, counts, tm):
    k, t = idx_t.shape
    n_tiles = t * k // tm
    n_visits = n_tiles + N_EXPERTS
    counts = counts.reshape(N_EXPERTS).astype(I32)
    ge = jnp.cumsum(counts)
    gs = ge - counts
    experts = jnp.arange(N_EXPERTS, dtype=I32)
    base = jnp.sum(jnp.where(idx_t[None] == experts[:, None, None], gs[:, None, None], 0), axis=0)
    pos = (base + c_t).T.reshape(t * k)
    first_tile = gs // tm
    n_vis = jnp.where(counts > 0, (ge - 1) // tm - first_tile + 1, 0)
    vend = jnp.cumsum(n_vis)
    vstart = vend - n_vis
    v = jnp.arange(n_visits, dtype=I32)
    active = v < vend[-1]
    e_of_v = jnp.minimum(jnp.sum(vend[None, :] <= v[:, None], axis=1), N_EXPERTS - 1).astype(I32)
    onehot = e_of_v[:, None] == experts[None, :]
    pick = lambda table: jnp.sum(jnp.where(onehot, table[None, :], 0), axis=1)
    tile_of_v = jnp.where(active, pick(first_tile) + v - pick(vstart), n_tiles - 1).astype(I32)
    prev_tile = jnp.concatenate([jnp.full((1,), -1, I32), tile_of_v[:-1]])
    flags = jnp.where(active, VISIT_ACTIVE + VISIT_FIRST * (tile_of_v != prev_tile), 0).astype(I32)
    return pos, (tile_of_v, e_of_v, flags, gs, ge)


def _prep_weights(w_in, b_forget, g_q, g_k, g_vnorm, b_vnorm, g_norm_mix, g_norm_ffn,
                  w_branch_a, w_branch_b, w_out, w_router, w_sh_gate, w_sh_up, w_sh_down):
    aw, sw, d = ATTN_WIDTH, SGU_WIDTH, D_MODEL
    o = 3 * aw + N_HEADS
    wf = jnp.zeros((d, LANES), F32).at[:, :N_HEADS].set(w_in[:, 3 * aw:o])
    bf = jnp.zeros((1, LANES), F32).at[0, :N_HEADS].set(b_forget)
    lane = jnp.arange(aw)
    bd = (lane[:, None] // HEAD_DIM == lane[None, :] // HEAD_DIM).astype(BF16)
    wr = jnp.zeros((d, LANES), F32).at[:, :N_EXPERTS].set(w_router)
    wrh = wr.astype(BF16)
    wrl = (wr - wrh.astype(F32)).astype(BF16)
    return dict(
        gmix=g_norm_mix.reshape(1, d), gffn=g_norm_ffn.reshape(1, d),
        wq=w_in[:, 0:aw].astype(BF16), wk=w_in[:, aw:2 * aw].astype(BF16),
        wv=w_in[:, 2 * aw:3 * aw].astype(BF16), wf=wf.astype(BF16), bf=bf,
        wu=w_in[:, o:o + sw].astype(BF16), wvg=w_in[:, o + sw:o + 2 * sw].astype(BF16),
        wga=w_in[:, o + 2 * sw:o + 2 * sw + d].astype(BF16),
        wgb=w_in[:, o + 2 * sw + d:o + 2 * sw + 2 * d].astype(BF16),
        gq=jnp.tile(g_q, N_HEADS).reshape(1, aw), gk=jnp.tile(g_k, N_HEADS).reshape(1, aw),
        gvn=g_vnorm.reshape(1, sw), bvn=b_vnorm.reshape(1, sw), bd=bd,
        wbb=w_branch_b.astype(BF16), wba=w_branch_a.astype(BF16), wo=w_out.astype(BF16),
        wrh=wrh, wrl=wrl, wsg=w_sh_gate.astype(BF16), wsu=w_sh_up.astype(BF16),
        wsd=w_sh_down.astype(BF16))


def _spatial_weights(w_spatial, b_spatial, rows_are_sequences, tl):
    if rows_are_sequences:
        wsp = w_spatial[:, 0, 0][:, None, None] * jnp.eye(CHUNK, dtype=F32)[None]
        b = jnp.broadcast_to(b_spatial[:, 0:1], (SGU_GROUPS, CHUNK))
    else:
        wsp = jnp.where(jnp.tril(jnp.ones((CHUNK, CHUNK), bool)), w_spatial, 0)
        b = b_spatial
    half = LANES // 2
    bsp = jnp.repeat(b.reshape(SGU_GROUPS // 2, 2, CHUNK), half, axis=1)
    bsp = bsp.transpose(0, 2, 1)
    r = jnp.arange(tl)
    ltri = (r[:, None] >= r[None, :]).astype(BF16)
    return dict(wsp=wsp.astype(BF16), bsp=bsp, ltri=ltri)


def _layer(x, mod, attend, wts, w_spatial, b_spatial, b_router, w_exp_gate, w_exp_up, w_exp_down,
           rows_are_sequences, tl, tm, tt_route, tt_disp, tt_comb):
    bx, l, d = x.shape
    wts = dict(wts, **_spatial_weights(w_spatial, b_spatial, rows_are_sequences, tl))
    q, kf, vf, kb, vb, lf, fc, mb, sga, vn = _mix_in(x, mod, wts, tl)
    oa = attend(q, kf, vf, kb, vb, lf, fc)
    h2, logits, base = _mix_out(x, oa, sga, mb, mod, wts, tl)
    t = bx * l
    idx_t, w_t, c_t, counts = _route(logits.reshape(t, LANES).T, b_router, tt_route)
    pos, plan = _dispatch_plan(idx_t, c_t, counts, tm)
    xs = _dispatch(pos, h2.reshape(t, d // 2), tt_disp)
    eo = _moe(plan, xs, w_exp_gate, w_exp_up, w_exp_down, tm)
    y = _combine(pos, eo, w_t.T.reshape(bx, l, TOP_K), base, mod, tt_comb)
    return y, kf, vf, lf, vn


def kernel(x_prompt, x_sample, c_prompt, c_sample, cache_k, cache_v, cache_logf, page_table, w_ada, b_ada, g_norm_mix, g_norm_ffn, w_in, b_forget, g_q, g_k, g_vnorm, b_vnorm, w_spatial, b_spatial, w_branch_a, w_branch_b, w_out, w_router, b_router, w_exp_gate, w_exp_up, w_exp_down, w_sh_gate, w_sh_up, w_sh_down):
    assert w_ada.shape[0] == 1, "one layer"
    b, s, d = x_prompt.shape
    nb = x_sample.shape[0]

    c_all = jnp.concatenate([c_prompt, c_sample], axis=0)
    pad = (-c_all.shape[0]) % 8
    c_all = jnp.pad(c_all, ((0, pad), (0, 0)))
    mod_all = _ada(c_all, w_ada[0], b_ada[0])
    mod_p = mod_all[:b].reshape(b, 1, 6 * d)
    mod_s = mod_all[b:b + nb].reshape(1, nb, 6 * d)

    wts = _prep_weights(w_in[0], b_forget[0], g_q[0], g_k[0], g_vnorm[0], b_vnorm[0], g_norm_mix[0],
                        g_norm_ffn[0], w_branch_a[0], w_branch_b[0], w_out[0], w_router[0],
                        w_sh_gate[0], w_sh_up[0], w_sh_down[0])
    experts = (w_exp_gate[0], w_exp_up[0], w_exp_down[0])

    tq = min(512, s)

    def attend_prompt(q, kf, vf, kb, vb, lf, fc):
        return _attn_prompt(q, kb, vb, fc.transpose(0, 2, 1), tq)

    def attend_sample(q, kf, vf, kb, vb, lf, fc):
        def lane_rep(a):
            a = a.astype(F32).reshape(nb, N_HEADS, HEAD_DIM, 1)
            return jnp.broadcast_to(a, (nb, N_HEADS, HEAD_DIM, PAGE))

        o = _attn_decode(page_table, lane_rep(q), lane_rep(kf), lane_rep(vf),
                         lf.reshape(nb, N_HEADS, 1),
                         cache_logf.transpose(0, 1, 3, 2),
                         cache_k.transpose(0, 1, 3, 4, 2),
                         cache_v.transpose(0, 1, 3, 4, 2))
        return o[..., 0].reshape(1, nb, ATTN_WIDTH).astype(BF16)

    y_s, k_s, v_s, lf_s, vn_s = _layer(x_sample.reshape(1, nb, d), mod_s, attend_sample, wts,
                                       w_spatial[0], b_spatial[0], b_router[0], *experts,
                                       rows_are_sequences=True, tl=nb, tm=32, tt_route=nb,
                                       tt_disp=nb, tt_comb=64)
    y_p, k_p, v_p, lf_p, _ = _layer(x_prompt, mod_p, attend_prompt, wts, w_spatial[0], b_spatial[0],
                                    b_router[0], *experts, rows_are_sequences=False,
                                    tl=min(512, s), tm=min(512, s), tt_route=min(1024, b * s),
                                    tt_disp=256, tt_comb=128)
    hd5 = (1, b, s, N_HEADS, HEAD_DIM)
    sd5 = (1, nb, 1, N_HEADS, HEAD_DIM)
    return (y_p, y_s.reshape(nb, 1, d),
            k_p.reshape(hd5), v_p.reshape(hd5), lf_p.reshape(1, b, s, N_HEADS),
            k_s.reshape(sd5), v_s.reshape(sd5), lf_s.reshape(1, nb, 1, N_HEADS),
            vn_s.reshape(1, nb, 1, SGU_WIDTH))
```

```python
import functools

import jax
import jax.numpy as jnp
from jax import lax
from jax.experimental import pallas as pl
from jax.experimental.pallas import tpu as pltpu

F32 = jnp.float32
BF16 = jnp.bfloat16
I32 = jnp.int32
U32 = jnp.uint32

D_MODEL = 1024
N_HEADS = 8
HEAD_DIM = 64
ATTN_WIDTH = N_HEADS * HEAD_DIM
SGU_GROUPS = 8
SGU_WIDTH = 512
CHUNK = 128
N_EXPERTS = 64
TOP_K = 8
N_EXPERT_GROUPS = 8
TOP_K_GROUPS = 4
EXPERTS_PER_GROUP = 8
D_EXPERT = 256
D_SHARED = 256
ROUTED_SCALE = 2.5
NORM_EPS = 1e-6
ATTN_SCALE = HEAD_DIM ** -0.5
LOG2E = 1.4426950408889634
PAGE = 128
LANES = 128
VMEM_LIMIT = 56 * 1024 * 1024
PAGES_PER_STEP = 16

_dot = functools.partial(jnp.dot, preferred_element_type=F32)


def _dot_nt(a, b):
    return lax.dot_general(a, b, (((1,), (1,)), ((), ())), preferred_element_type=F32)


def _sigmoid(x):
    return 1.0 / (1.0 + jnp.exp(-x))


def _gelu(x):
    return 0.5 * x * (1.0 + jnp.tanh(0.7978845608028654 * (x + 0.044715 * (x * x * x))))


def _split3(x):
    hi = x.astype(BF16)
    r1 = x - hi.astype(F32)
    mid = r1.astype(BF16)
    lo = (r1 - mid.astype(F32)).astype(BF16)
    return hi, mid, lo


def _pack_halves(x):
    w = x.shape[1] // 2
    lo = lax.bitcast_convert_type(x[:, :w].astype(BF16).astype(F32), U32)
    hi = lax.bitcast_convert_type(x[:, w:].astype(BF16).astype(F32), U32)
    return (hi & jnp.uint32(0xFFFF0000)) | (lo >> 16)


def _unpack_halves(p):
    return (lax.bitcast_convert_type(p << 16, F32),
            lax.bitcast_convert_type(p & jnp.uint32(0xFFFF0000), F32))


def _dot3_left(m_bf16, x):
    hi, mid, lo = _split3(x)
    return _dot(m_bf16, hi) + _dot(m_bf16, mid) + _dot(m_bf16, lo)


def _dot3_right(x, m_bf16):
    hi, mid, lo = _split3(x)
    return _dot(hi, m_bf16) + _dot(mid, m_bf16) + _dot(lo, m_bf16)


def _params(*sem, row_dma=False):
    return pltpu.CompilerParams(dimension_semantics=sem, vmem_limit_bytes=VMEM_LIMIT,
                                disable_bounds_checks=row_dma)


def _const_spec(shape):
    zeros = (0,) * len(shape)
    return pl.BlockSpec(shape, lambda *_: zeros)


def _ada_kernel(c_ref, w_ref, b_ref, o_ref):
    c = c_ref[...]
    a = c * _sigmoid(c)
    o_ref[...] = jnp.dot(a, w_ref[...], preferred_element_type=F32,
                         precision=lax.Precision.HIGHEST) + b_ref[...]


def _ada(c, w_ada, b_ada):
    m, d = c.shape
    n = w_ada.shape[1]
    tn = 1024
    return pl.pallas_call(
        _ada_kernel,
        grid=(n // tn,),
        in_specs=[pl.BlockSpec((m, d), lambda j: (0, 0)),
                  pl.BlockSpec((d, tn), lambda j: (0, j)),
                  pl.BlockSpec((1, tn), lambda j: (0, j))],
        out_specs=pl.BlockSpec((m, tn), lambda j: (0, j)),
        out_shape=jax.ShapeDtypeStruct((m, n), F32),
        compiler_params=_params("arbitrary"),
        name="ada",
    )(c, w_ada, b_ada.reshape(1, n))


def _mix_in_kernel(x_ref, mod_ref, gmix_ref, wq_ref, wk_ref, wv_ref, wf_ref, wu_ref, wvg_ref,
                   wga_ref, wgb_ref, bf_ref, gq_ref, gk_ref, gvn_ref, bvn_ref, bd_ref,
                   wsp_ref, bsp_ref, wbb_ref, ltri_ref,
                   q_ref, kf_ref, vf_ref, kb_ref, vb_ref, lf_ref, fc_ref, mb_ref, sga_ref, vn_ref,
                   carry_sc):
    tl = x_ref.shape[1]
    x = x_ref[0]
    shift1 = mod_ref[0, :, 0:D_MODEL]
    scale1 = mod_ref[0, :, D_MODEL:2 * D_MODEL]
    ms = jnp.mean(x * x, axis=-1, keepdims=True)
    h = x * lax.rsqrt(ms + NORM_EPS) * gmix_ref[...] * (1.0 + scale1) + shift1
    hb = h.astype(BF16)
    bd = bd_ref[...]

    def head_norm(z, g):
        ss = _dot((z * z).astype(BF16), bd) * (1.0 / HEAD_DIM)
        return z * lax.rsqrt(ss + NORM_EPS) * g

    qn = head_norm(_dot(hb, wq_ref[...]), gq_ref[...]) * (ATTN_SCALE * LOG2E)
    q_ref[0] = qn.astype(BF16)
    kn = head_norm(_dot(hb, wk_ref[...]), gk_ref[...])
    kf_ref[0] = kn
    kb_ref[0] = kn.astype(BF16)
    v = _dot(hb, wv_ref[...])
    vf_ref[0] = v
    vb_ref[0] = v.astype(BF16)

    zf = _dot(hb, wf_ref[...]) + bf_ref[...]
    lf = jnp.minimum(zf, 0.0) - jnp.log(1.0 + jnp.exp(-jnp.abs(zf)))
    lf_ref[0] = lf[:, :N_HEADS]

    @pl.when(pl.program_id(1) == 0)
    def _():
        carry_sc[...] = jnp.zeros_like(carry_sc)

    fc = _dot3_left(ltri_ref[...], lf) + carry_sc[...]
    fc_ref[0] = fc[:, :N_HEADS]
    carry_sc[...] = fc[tl - 1:tl, :]

    gu = _gelu(_dot(hb, wu_ref[...]))
    gv = _gelu(_dot(hb, wvg_ref[...]))
    mu = jnp.mean(gv, axis=-1, keepdims=True)
    gc = gv - mu
    var = jnp.mean(gc * gc, axis=-1, keepdims=True)
    vn = gc * lax.rsqrt(var + NORM_EPS) * gvn_ref[...] + bvn_ref[...]
    vn_ref[0] = vn
    vnb = vn.astype(BF16)
    lane = lax.broadcasted_iota(I32, (CHUNK, LANES), 1)
    low = lane < (LANES // 2)
    zero = jnp.zeros((CHUNK, LANES), BF16)
    rows = []
    for c in range(tl // CHUNK):
        pieces = []
        for j in range(SGU_WIDTH // LANES):
            vp = vnb[c * CHUNK:(c + 1) * CHUNK, j * LANES:(j + 1) * LANES]
            mixed = (_dot(wsp_ref[2 * j], jnp.where(low, vp, zero))
                     + _dot(wsp_ref[2 * j + 1], jnp.where(low, zero, vp)) + bsp_ref[j])
            pieces.append(mixed)
        rows.append(jnp.concatenate(pieces, axis=1))
    mixed = rows[0] if len(rows) == 1 else jnp.concatenate(rows, axis=0)
    ob = (gu * mixed).astype(BF16)
    mb = _sigmoid(_dot(hb, wgb_ref[...])) * _dot(ob, wbb_ref[...])
    mb_ref[0] = mb.astype(BF16)
    sga_ref[0] = _sigmoid(_dot(hb, wga_ref[...])).astype(BF16)


def _mix_in(x, mod, wts, tl):
    bx, l, d = x.shape
    tlm = tl if mod.shape[1] > 1 else 1
    grid = (bx, l // tl)
    row = lambda b, i: (b, i, 0)
    mod_map = row if tlm > 1 else (lambda b, i: (b, 0, 0))
    names = ("gmix", "wq", "wk", "wv", "wf", "wu", "wvg", "wga", "wgb", "bf", "gq", "gk", "gvn",
             "bvn", "bd", "wsp", "bsp", "wbb", "ltri")
    consts = [wts[n] for n in names]
    out_widths = [(ATTN_WIDTH, BF16), (ATTN_WIDTH, F32), (ATTN_WIDTH, F32), (ATTN_WIDTH, BF16),
                  (ATTN_WIDTH, BF16), (N_HEADS, F32), (N_HEADS, F32), (D_MODEL, BF16),
                  (D_MODEL, BF16), (SGU_WIDTH, F32)]
    return pl.pallas_call(
        _mix_in_kernel,
        grid=grid,
        in_specs=[pl.BlockSpec((1, tl, d), row), pl.BlockSpec((1, tlm, 6 * d), mod_map)]
                 + [_const_spec(c.shape) for c in consts],
        out_specs=[pl.BlockSpec((1, tl, w), row) for w, _ in out_widths],
        out_shape=[jax.ShapeDtypeStruct((bx, l, w), dt) for w, dt in out_widths],
        scratch_shapes=[pltpu.VMEM((1, LANES), F32)],
        compiler_params=_params("arbitrary", "arbitrary"),
        name="mix_in",
    )(x, mod, *consts)


def _attn_kernel(qi_ref, ki_ref, q_ref, k_ref, v_ref, fq_ref, fk_ref, o_ref, m_sc, l_sc, acc_sc):
    tq = q_ref.shape[1]
    tk = k_ref.shape[1]
    pair = pl.program_id(1)
    qi = qi_ref[pair]
    ki = ki_ref[pair]

    @pl.when(ki == 0)
    def _():
        m_sc[...] = jnp.full_like(m_sc, -jnp.inf)
        l_sc[...] = jnp.zeros_like(l_sc)
        acc_sc[...] = jnp.zeros_like(acc_sc)

    lane = lax.broadcasted_iota(I32, (tq, LANES), 1)
    low = lane < HEAD_DIM

    def step(masked):
        ones = jnp.ones((tk, LANES), BF16)
        if masked:
            causal = (lax.broadcasted_iota(I32, (tq, tk), 0) >= lax.broadcasted_iota(I32, (tq, tk), 1))
        for j in range(ATTN_WIDTH // LANES):
            sl = slice(j * LANES, (j + 1) * LANES)
            qp = q_ref[0, :, sl]
            kp = k_ref[0, :, sl]
            vx = jnp.concatenate([v_ref[0, :, sl], ones], axis=1)
            zero = jnp.zeros_like(qp)
            alphas = []
            pvs = []
            for t in range(2):
                hd = 2 * j + t
                qh = jnp.where(low, qp, zero) if t == 0 else jnp.where(low, zero, qp)
                decay = (fq_ref[0, hd:hd + 1, 0:1] - fk_ref[0, hd:hd + 1, :]) * LOG2E
                s = _dot_nt(qh, kp) + decay
                if masked:
                    s = jnp.where(causal, s, -jnp.inf)
                m_prev = m_sc[hd]
                m_new = jnp.maximum(m_prev, jnp.max(s, axis=-1, keepdims=True))
                alpha = jnp.exp2(m_prev - m_new)
                p = jnp.concatenate([jnp.exp2(s[:, c * LANES:(c + 1) * LANES] - m_new)
                                     for c in range(tk // LANES)], axis=1)
                pv = _dot(p.astype(BF16), vx)
                l_sc[hd] = alpha * l_sc[hd] + pv[:, LANES:]
                m_sc[hd] = m_new
                alphas.append(alpha)
                pvs.append(pv[:, :LANES])
            acc_sc[j] = (acc_sc[j] * jnp.where(low, alphas[0], alphas[1])
                         + jnp.where(low, pvs[0], pvs[1]))

    @pl.when(ki < qi)
    def _():
        step(False)

    @pl.when(ki == qi)
    def _():
        step(True)
        for j in range(ATTN_WIDTH // LANES):
            inv = jnp.where(low, 1.0 / l_sc[2 * j], 1.0 / l_sc[2 * j + 1])
            o_ref[0, :, j * LANES:(j + 1) * LANES] = (acc_sc[j] * inv).astype(o_ref.dtype)


def _attn_prompt(q, k, v, fr, tq):
    b, s, w = q.shape
    nq = s // tq
    pairs = [(qi, ki) for qi in range(nq) for ki in range(qi + 1)]
    qi_arr = jnp.asarray([p[0] for p in pairs], I32)
    ki_arr = jnp.asarray([p[1] for p in pairs], I32)
    qmap = lambda bi, p, qa, ka: (bi, qa[p], 0)
    kmap = lambda bi, p, qa, ka: (bi, ka[p], 0)
    grid_spec = pltpu.PrefetchScalarGridSpec(
        num_scalar_prefetch=2,
        grid=(b, len(pairs)),
        in_specs=[pl.BlockSpec((1, tq, w), qmap),
                  pl.BlockSpec((1, tq, w), kmap),
                  pl.BlockSpec((1, tq, w), kmap),
                  pl.BlockSpec((1, N_HEADS, tq), lambda bi, p, qa, ka: (bi, 0, qa[p])),
                  pl.BlockSpec((1, N_HEADS, tq), lambda bi, p, qa, ka: (bi, 0, ka[p]))],
        out_specs=pl.BlockSpec((1, tq, w), qmap),
        scratch_shapes=[pltpu.VMEM((N_HEADS, tq, LANES), F32), pltpu.VMEM((N_HEADS, tq, LANES), F32),
                        pltpu.VMEM((w // LANES, tq, LANES), F32)],
    )
    return pl.pallas_call(
        _attn_kernel,
        grid_spec=grid_spec,
        out_shape=jax.ShapeDtypeStruct((b, s, w), BF16),
        compiler_params=_params("arbitrary", "arbitrary"),
        name="attn_prompt",
    )(qi_arr, ki_arr, q, k, v, fr, fr)


def _rows_to_tile(rows):
    sub = lax.broadcasted_iota(I32, (N_HEADS, PAGE), 0)
    tile = jnp.zeros((N_HEADS, PAGE), F32)
    for h, r in enumerate(rows):
        tile = jnp.where(sub == h, r, tile)
    return tile


def _attn_decode_kernel(pt_ref, q_ref, kn_ref, vn_ref, lfn_ref, u_ref, *refs):
    n = PAGES_PER_STEP
    lf_refs, k_refs, v_refs = refs[:n], refs[n:2 * n], refs[2 * n:3 * n]
    o_ref, m_sc, l_sc, acc_sc, carry_sc = refs[3 * n:]
    j = pl.program_id(1)
    nj = pl.num_programs(1)
    heads = range(N_HEADS)

    @pl.when(j == 0)
    def _():
        m_sc[...] = jnp.full_like(m_sc, -jnp.inf)
        l_sc[...] = jnp.zeros_like(l_sc)
        acc_sc[...] = jnp.zeros_like(acc_sc)
        carry_sc[...] = jnp.zeros_like(carry_sc)

    lfn = lfn_ref[0]
    lf_all = jnp.concatenate([lf_refs[i][0, 0] for i in range(n)], axis=0)
    later_all = _dot3_right(lf_all, u_ref[...])
    carry = carry_sc[...]
    decays = []
    for i in range(n):
        sl = slice(i * N_HEADS, (i + 1) * N_HEADS)
        decays.append((later_all[sl] + carry + lfn) * LOG2E)
        carry = carry + jnp.sum(lf_all[sl], axis=-1, keepdims=True)
    carry_sc[...] = carry

    rows = [[None] * N_HEADS for _ in range(n)]
    for h in heads:
        qh = q_ref[0, h]
        for i in range(n):
            rows[i][h] = jnp.sum(k_refs[i][0, 0, h] * qh, axis=0, keepdims=True)
    s = [_rows_to_tile(rows[i]) + decays[i] for i in range(n)]
    m = m_sc[...]
    m_new = m
    for i in range(n):
        m_new = jnp.maximum(m_new, jnp.max(s[i], axis=-1, keepdims=True))
    alpha = jnp.exp2(m - m_new)
    p = [jnp.exp2(s[i] - m_new) for i in range(n)]
    l = alpha * l_sc[...]
    for i in range(n):
        l = l + jnp.sum(p[i], axis=-1, keepdims=True)
    m_sc[...] = m_new
    l_sc[...] = l
    alpha_rep = jnp.broadcast_to(alpha, (N_HEADS, PAGE))
    for h in heads:
        a = acc_sc[h] * alpha_rep[h:h + 1, :]
        for i in range(n):
            a = a + p[i][h:h + 1, :] * v_refs[i][0, 0, h]
        acc_sc[h] = a

    @pl.when(j == nj - 1)
    def _():
        s_n = _rows_to_tile([jnp.sum(q_ref[0, h] * kn_ref[0, h], axis=0, keepdims=True) for h in heads])
        m_rep = jnp.broadcast_to(m_new, (N_HEADS, PAGE))
        m_fin = jnp.maximum(m_rep, s_n)
        a_fin = jnp.exp2(m_rep - m_fin)
        p_n = jnp.exp2(s_n - m_fin)
        inv = 1.0 / (a_fin * jnp.broadcast_to(l, (N_HEADS, PAGE)) + p_n)
        for h in heads:
            total = jnp.broadcast_to(jnp.sum(acc_sc[h], axis=-1, keepdims=True), (HEAD_DIM, PAGE))
            o_ref[0, h] = (a_fin[h:h + 1, :] * total + p_n[h:h + 1, :] * vn_ref[0, h]) * inv[h:h + 1, :]


def _attn_decode(page_table, q_rep, k_new_rep, v_new_rep, lf_new, cache_lf_t, cache_k_t, cache_v_t):
    nb, n_pages = page_table.shape
    n = PAGES_PER_STEP
    tok3 = lambda b, j, pt: (b, 0, 0)
    tok4 = lambda b, j, pt: (b, 0, 0, 0)

    def page(i, rank):
        def index_map(b, j, pt):
            return (0, pt[b * n_pages + n_pages - 1 - (j * n + i)]) + (0,) * (rank - 2)
        return index_map

    lane = jnp.arange(PAGE)
    later = (lane[:, None] > lane[None, :]).astype(BF16)
    rep = pl.BlockSpec((1, N_HEADS, HEAD_DIM, PAGE), tok4)
    grid_spec = pltpu.PrefetchScalarGridSpec(
        num_scalar_prefetch=1,
        grid=(nb, n_pages // n),
        in_specs=[rep, rep, rep,
                  pl.BlockSpec((1, N_HEADS, 1), tok3),
                  pl.BlockSpec((PAGE, PAGE), lambda b, j, pt: (0, 0))]
                 + [pl.BlockSpec((1, 1, N_HEADS, PAGE), page(i, 4)) for i in range(n)]
                 + [pl.BlockSpec((1, 1, N_HEADS, HEAD_DIM, PAGE), page(i, 5)) for i in range(n)]
                 + [pl.BlockSpec((1, 1, N_HEADS, HEAD_DIM, PAGE), page(i, 5)) for i in range(n)],
        out_specs=rep,
        scratch_shapes=[pltpu.VMEM((N_HEADS, 1), F32), pltpu.VMEM((N_HEADS, 1), F32),
                        pltpu.VMEM((N_HEADS, HEAD_DIM, PAGE), F32), pltpu.VMEM((N_HEADS, 1), F32)],
    )
    return pl.pallas_call(
        _attn_decode_kernel,
        grid_spec=grid_spec,
        out_shape=jax.ShapeDtypeStruct((nb, N_HEADS, HEAD_DIM, PAGE), F32),
        compiler_params=_params("arbitrary", "arbitrary"),
        name="attn_decode",
    )(page_table.reshape(-1), q_rep, k_new_rep, v_new_rep, lf_new, later,
      *([cache_lf_t] * n), *([cache_k_t] * n), *([cache_v_t] * n))


def _mix_out_kernel(x_ref, oa_ref, sga_ref, mb_ref, mod_ref, wba_ref, wo_ref, gffn_ref,
                    wrh_ref, wrl_ref, wsg_ref, wsu_ref, wsd_ref,
                    h2_ref, lg_ref, base_ref):
    d = D_MODEL
    x = x_ref[0]
    gate1 = mod_ref[0, :, 2 * d:3 * d]
    shift2 = mod_ref[0, :, 3 * d:4 * d]
    scale2 = mod_ref[0, :, 4 * d:5 * d]
    gate2 = mod_ref[0, :, 5 * d:6 * d]
    merged = sga_ref[0].astype(F32) * _dot(oa_ref[0], wba_ref[...]) + mb_ref[0].astype(F32)
    x1 = x + gate1 * _dot(merged.astype(BF16), wo_ref[...])
    ms = jnp.mean(x1 * x1, axis=-1, keepdims=True)
    h2 = x1 * lax.rsqrt(ms + NORM_EPS) * gffn_ref[...] * (1.0 + scale2) + shift2
    h2_ref[0] = _pack_halves(h2)
    hb = h2.astype(BF16)
    hl = (h2 - hb.astype(F32)).astype(BF16)
    lg_ref[0] = _dot(hb, wrh_ref[...]) + (_dot(hb, wrl_ref[...]) + _dot(hl, wrh_ref[...]))
    g = _dot(hb, wsg_ref[...])
    u = _dot(hb, wsu_ref[...])
    a = (g * _sigmoid(g) * u).astype(BF16)
    base_ref[0] = x1 + gate2 * _dot(a, wsd_ref[...])


def _mix_out(x, oa, sga, mb, mod, wts, tl):
    bx, l, d = x.shape
    tlm = tl if mod.shape[1] > 1 else 1
    row = lambda b, i: (b, i, 0)
    mod_map = row if tlm > 1 else (lambda b, i: (b, 0, 0))
    names = ("wba", "wo", "gffn", "wrh", "wrl", "wsg", "wsu", "wsd")
    consts = [wts[n] for n in names]
    return pl.pallas_call(
        _mix_out_kernel,
        grid=(bx, l // tl),
        in_specs=[pl.BlockSpec((1, tl, d), row), pl.BlockSpec((1, tl, ATTN_WIDTH), row),
                  pl.BlockSpec((1, tl, d), row), pl.BlockSpec((1, tl, d), row),
                  pl.BlockSpec((1, tlm, 6 * d), mod_map)] + [_const_spec(c.shape) for c in consts],
        out_specs=[pl.BlockSpec((1, tl, d // 2), row), pl.BlockSpec((1, tl, LANES), row),
                   pl.BlockSpec((1, tl, d), row)],
        out_shape=[jax.ShapeDtypeStruct((bx, l, d // 2), U32), jax.ShapeDtypeStruct((bx, l, LANES), F32),
                   jax.ShapeDtypeStruct((bx, l, d), F32)],
        compiler_params=_params("arbitrary", "arbitrary"),
        name="mix_out",
    )(x, oa, sga, mb, mod, *consts)


def _route_kernel(lg_ref, b_ref, before_ref, idx_ref, w_ref, c_ref, cnt_ref):
    tt = lg_ref.shape[1]
    epg = EXPERTS_PER_GROUP
    ninf = -jnp.inf
    iota = lax.broadcasted_iota(I32, (epg, tt), 0)
    sc = []
    biased = []
    gscore = []
    for g in range(N_EXPERT_GROUPS):
        s = _sigmoid(lg_ref[g * epg:(g + 1) * epg, :])
        bz = s + b_ref[g * epg:(g + 1) * epg, :]
        m1 = jnp.max(bz, axis=0, keepdims=True)
        first = jnp.min(jnp.where(bz == m1, iota, epg), axis=0, keepdims=True)
        m2 = jnp.max(jnp.where(iota == first, ninf, bz), axis=0, keepdims=True)
        sc.append(s)
        biased.append(bz)
        gscore.append(m1 + m2)
    cand = []
    for g in range(N_EXPERT_GROUPS):
        rank = jnp.zeros((1, tt), I32)
        for o in range(N_EXPERT_GROUPS):
            if o == g:
                continue
            beats = (gscore[o] >= gscore[g]) if o < g else (gscore[o] > gscore[g])
            rank = rank + beats.astype(I32)
        cand.append(jnp.where(rank < TOP_K_GROUPS, biased[g], ninf))
    ws = []
    picks = []
    for k in range(TOP_K):
        mx = cand[0]
        for g in range(1, N_EXPERT_GROUPS):
            mx = jnp.maximum(mx, cand[g])
        mx = jnp.max(mx, axis=0, keepdims=True)
        fi = jnp.where(cand[0] == mx, iota, N_EXPERTS)
        for g in range(1, N_EXPERT_GROUPS):
            fi = jnp.minimum(fi, jnp.where(cand[g] == mx, iota + g * epg, N_EXPERTS))
        fi = jnp.min(fi, axis=0, keepdims=True)
        wk = jnp.zeros((epg, tt), F32)
        for g in range(N_EXPERT_GROUPS):
            hit = (iota + g * epg) == fi
            wk = wk + jnp.where(hit, sc[g], 0.0)
            cand[g] = jnp.where(hit, ninf, cand[g])
        idx_ref[k:k + 1, :] = fi
        picks.append(fi)
        ws.append(jnp.sum(wk, axis=0, keepdims=True))
    tot = ws[0]
    for k in range(1, TOP_K):
        tot = tot + ws[k]
    for k in range(TOP_K):
        w_ref[k:k + 1, :] = ws[k] / tot * ROUTED_SCALE

    @pl.when(pl.program_id(0) == 0)
    def _():
        cnt_ref[...] = jnp.zeros_like(cnt_ref)

    chosen = []
    for g in range(N_EXPERT_GROUPS):
        sel = jnp.zeros((epg, tt), F32)
        for k in range(TOP_K):
            sel = sel + jnp.where((iota + g * epg) == picks[k], 1.0, 0.0)
        chosen.append(sel)
    chosen = jnp.concatenate(chosen, axis=0)
    earlier = _dot(chosen.astype(BF16), before_ref[...]) + cnt_ref[...]
    for k in range(TOP_K):
        ck = jnp.zeros((epg, tt), F32)
        for g in range(N_EXPERT_GROUPS):
            ck = ck + jnp.where((iota + g * epg) == picks[k], earlier[g * epg:(g + 1) * epg, :], 0.0)
        c_ref[k:k + 1, :] = jnp.sum(ck, axis=0, keepdims=True).astype(I32)
    cnt_ref[...] += jnp.sum(chosen, axis=1, keepdims=True)


def _route(logits_t, b_router, tt):
    t = logits_t.shape[1]
    r = jnp.arange(tt)
    before = (r[:, None] < r[None, :]).astype(BF16)
    col = lambda i: (0, i)
    return pl.pallas_call(
        _route_kernel,
        grid=(t // tt,),
        in_specs=[pl.BlockSpec((N_EXPERTS, tt), col),
                  pl.BlockSpec((N_EXPERTS, 1), lambda i: (0, 0)),
                  pl.BlockSpec((tt, tt), lambda i: (0, 0))],
        out_specs=[pl.BlockSpec((TOP_K, tt), col), pl.BlockSpec((TOP_K, tt), col),
                   pl.BlockSpec((TOP_K, tt), col), pl.BlockSpec((N_EXPERTS, 1), lambda i: (0, 0))],
        out_shape=[jax.ShapeDtypeStruct((TOP_K, t), I32), jax.ShapeDtypeStruct((TOP_K, t), F32),
                   jax.ShapeDtypeStruct((TOP_K, t), I32), jax.ShapeDtypeStruct((N_EXPERTS, 1), F32)],
        compiler_params=_params("arbitrary"),
        name="route",
    )(logits_t, b_router.reshape(N_EXPERTS, 1), before)


def _dispatch_kernel(pos_ref, h_ref, xs_hbm, sem):
    tt = h_ref.shape[0]

    def copy(j, k):
        return pltpu.make_async_copy(h_ref.at[pl.ds(j, 1)],
                                     xs_hbm.at[pl.ds(pos_ref[0, 0, j * TOP_K + k], 1)], sem.at[0])

    def issue(j, c):
        for k in range(TOP_K):
            copy(j, k).start()
        return c

    def wait(j, c):
        for k in range(TOP_K):
            copy(j, k).wait()
        return c

    lax.fori_loop(0, tt, issue, 0, unroll=2)
    lax.fori_loop(0, tt, wait, 0, unroll=2)


def _dispatch(pos, h2, tt):
    t, d = h2.shape
    n = t // tt
    return pl.pallas_call(
        _dispatch_kernel,
        grid=(n,),
        in_specs=[pl.BlockSpec((1, 1, tt * TOP_K), lambda i: (i, 0, 0), memory_space=pltpu.SMEM),
                  pl.BlockSpec((tt, d), lambda i: (i, 0))],
        out_specs=pl.BlockSpec(memory_space=pl.ANY),
        out_shape=jax.ShapeDtypeStruct((t * TOP_K, d), h2.dtype),
        scratch_shapes=[pltpu.SemaphoreType.DMA((1,))],
        compiler_params=_params("arbitrary", row_dma=True),
        name="moe_dispatch",
    )(pos.reshape(n, 1, tt * TOP_K), h2)


VISIT_ACTIVE = 1
VISIT_FIRST = 2


def _moe_kernel(vt_ref, ve_ref, vf_ref, gs_ref, ge_ref, x_ref, wg_ref, wu_ref, wd_ref, o_ref):
    v = pl.program_id(0)
    tm = x_ref.shape[0]
    flags = vf_ref[v]

    @pl.when(flags >= VISIT_ACTIVE)
    def _():
        e = ve_ref[v]
        x = jnp.concatenate(_unpack_halves(x_ref[...]), axis=1).astype(BF16)
        g = _dot(x, wg_ref[0].astype(BF16))
        u = _dot(x, wu_ref[0].astype(BF16))
        a = (g * _sigmoid(g) * u).astype(BF16)
        res = _pack_halves(_dot(a, wd_ref[0].astype(BF16)))
        row = vt_ref[v] * tm + lax.broadcasted_iota(I32, (tm, 1), 0)
        mine = (row >= gs_ref[e]) & (row < ge_ref[e])

        @pl.when(flags >= VISIT_FIRST)
        def _():
            o_ref[...] = jnp.where(mine, res, jnp.zeros_like(res))

        @pl.when(flags < VISIT_FIRST)
        def _():
            o_ref[...] = jnp.where(mine, res, o_ref[...])


def _moe(plan, xs, w_gate, w_up, w_down, tm):
    visit_tile, visit_e, visit_flags, gs, ge = plan
    n_visits = visit_tile.shape[0]
    rows, dp = xs.shape
    d = 2 * dp
    tile = lambda v, vt, ve, vf, s, e: (vt[v], 0)
    expert = lambda v, vt, ve, vf, s, e: (ve[v], 0, 0)
    grid_spec = pltpu.PrefetchScalarGridSpec(
        num_scalar_prefetch=5,
        grid=(n_visits,),
        in_specs=[pl.BlockSpec((tm, dp), tile),
                  pl.BlockSpec((1, d, D_EXPERT), expert),
                  pl.BlockSpec((1, d, D_EXPERT), expert),
                  pl.BlockSpec((1, D_EXPERT, d), expert)],
        out_specs=pl.BlockSpec((tm, dp), tile),
    )
    return pl.pallas_call(
        _moe_kernel,
        grid_spec=grid_spec,
        out_shape=jax.ShapeDtypeStruct((rows, dp), U32),
        compiler_params=_params("arbitrary"),
        name="moe_experts",
    )(visit_tile, visit_e, visit_flags, gs, ge, xs, w_gate, w_up, w_down)


def _combine_kernel(pos_ref, posn_ref, eo_hbm, w_ref, base_ref, mod_ref, y_ref, buf, sem):
    tt = base_ref.shape[1]
    i = pl.program_id(0) * pl.num_programs(1) + pl.program_id(1)
    n = pl.num_programs(0) * pl.num_programs(1)
    slot = i % 2

    def copy(idx_ref, j, k, s):
        return pltpu.make_async_copy(eo_hbm.at[pl.ds(idx_ref[0, 0, j * TOP_K + k], 1)],
                                     buf.at[s, k, pl.ds(j, 1)], sem.at[s])

    def issue(idx_ref, s):
        def body(j, c):
            for k in range(TOP_K):
                copy(idx_ref, j, k, s).start()
            return c
        lax.fori_loop(0, tt, body, 0, unroll=2)

    @pl.when(i == 0)
    def _():
        issue(pos_ref, 0)

    @pl.when(i + 1 < n)
    def _():
        issue(posn_ref, 1 - slot)

    def wait(j, c):
        for k in range(TOP_K):
            copy(pos_ref, j, k, slot).wait()
        return c
    lax.fori_loop(0, tt, wait, 0, unroll=2)

    w = w_ref[0]
    half = D_MODEL // 2
    lo, hi = _unpack_halves(buf[slot, 0])
    lo, hi = lo * w[:, 0:1], hi * w[:, 0:1]
    for k in range(1, TOP_K):
        lo_k, hi_k = _unpack_halves(buf[slot, k])
        lo, hi = lo + lo_k * w[:, k:k + 1], hi + hi_k * w[:, k:k + 1]
    gate2 = mod_ref[0, :, 5 * D_MODEL:6 * D_MODEL]
    y_ref[0, :, :half] = base_ref[0, :, :half] + gate2[:, :half] * lo
    y_ref[0, :, half:] = base_ref[0, :, half:] + gate2[:, half:] * hi


def _combine(pos, eo, w, base, mod, tt):
    bx, l, d = base.shape
    nl = l // tt
    n = bx * nl
    pos3 = pos.reshape(n, 1, tt * TOP_K)
    tlm = tt if mod.shape[1] > 1 else 1
    row = lambda b, i: (b, i, 0)
    mod_map = row if tlm > 1 else (lambda b, i: (b, 0, 0))
    return pl.pallas_call(
        _combine_kernel,
        grid=(bx, nl),
        in_specs=[pl.BlockSpec((1, 1, tt * TOP_K), lambda b, i: (b * nl + i, 0, 0), memory_space=pltpu.SMEM),
                  pl.BlockSpec((1, 1, tt * TOP_K), lambda b, i: (jnp.minimum(b * nl + i + 1, n - 1), 0, 0),
                               memory_space=pltpu.SMEM),
                  pl.BlockSpec(memory_space=pl.ANY),
                  pl.BlockSpec((1, tt, TOP_K), row),
                  pl.BlockSpec((1, tt, d), row),
                  pl.BlockSpec((1, tlm, 6 * d), mod_map)],
        out_specs=pl.BlockSpec((1, tt, d), row),
        out_shape=jax.ShapeDtypeStruct((bx, l, d), F32),
        scratch_shapes=[pltpu.VMEM((2, TOP_K, tt, d // 2), U32), pltpu.SemaphoreType.DMA((2,))],
        compiler_params=_params("arbitrary", "arbitrary", row_dma=True),
        name="moe_combine",
    )(pos3, pos3, eo, w, base, mod)


def _dispatch_plan(idx_t, c_t, counts, tm):
    k, t = idx_t.shape
    n_tiles = t * k // tm
    n_visits = n_tiles + N_EXPERTS
    counts = counts.reshape(N_EXPERTS).astype(I32)
    ge = jnp.cumsum(counts)
    gs = ge - counts
    experts = jnp.arange(N_EXPERTS, dtype=I32)
    base = jnp.sum(jnp.where(idx_t[None] == experts[:, None, None], gs[:, None, None], 0), axis=0)
    pos = (base + c_t).T.reshape(t * k)
    first_tile = gs // tm
    n_vis = jnp.where(counts > 0, (ge - 1) // tm - first_tile + 1, 0)
    vend = jnp.cumsum(n_vis)
    vstart = vend - n_vis
    v = jnp.arange(n_visits, dtype=I32)
    active = v < vend[-1]
    e_of_v = jnp.minimum(jnp.sum(vend[None, :] <= v[:, None], axis=1), N_EXPERTS - 1).astype(I32)
    onehot = e_of_v[:, None] == experts[None, :]
    pick = lambda table: jnp.sum(jnp.where(onehot, table[None, :], 0), axis=1)
    tile_of_v = jnp.where(active, pick(first_tile) + v - pick(vstart), n_tiles - 1).astype(I32)
    prev_tile = jnp.concatenate([jnp.full((1,), -1, I32), tile_of_v[:-1]])
    flags = jnp.where(active, VISIT_ACTIVE + VISIT_FIRST * (tile_of_v != prev_tile), 0).astype(I32)
    return pos, (tile_of_v, e_of_v, flags, gs, ge)


def _prep_weights(w_in, b_forget, g_q, g_k, g_vnorm, b_vnorm, g_norm_mix, g_norm_ffn,
                  w_branch_a, w_branch_b, w_out, w_router, w_sh_gate, w_sh_up, w_sh_down):
    aw, sw, d = ATTN_WIDTH, SGU_WIDTH, D_MODEL
    o = 3 * aw + N_HEADS
    wf = jnp.zeros((d, LANES), F32).at[:, :N_HEADS].set(w_in[:, 3 * aw:o])
    bf = jnp.zeros((1, LANES), F32).at[0, :N_HEADS].set(b_forget)
    lane = jnp.arange(aw)
    bd = (lane[:, None] // HEAD_DIM == lane[None, :] // HEAD_DIM).astype(BF16)
    wr = jnp.zeros((d, LANES), F32).at[:, :N_EXPERTS].set(w_router)
    wrh = wr.astype(BF16)
    wrl = (wr - wrh.astype(F32)).astype(BF16)
    return dict(
        gmix=g_norm_mix.reshape(1, d), gffn=g_norm_ffn.reshape(1, d),
        wq=w_in[:, 0:aw].astype(BF16), wk=w_in[:, aw:2 * aw].astype(BF16),
        wv=w_in[:, 2 * aw:3 * aw].astype(BF16), wf=wf.astype(BF16), bf=bf,
        wu=w_in[:, o:o + sw].astype(BF16), wvg=w_in[:, o + sw:o + 2 * sw].astype(BF16),
        wga=w_in[:, o + 2 * sw:o + 2 * sw + d].astype(BF16),
        wgb=w_in[:, o + 2 * sw + d:o + 2 * sw + 2 * d].astype(BF16),
        gq=jnp.tile(g_q, N_HEADS).reshape(1, aw), gk=jnp.tile(g_k, N_HEADS).reshape(1, aw),
        gvn=g_vnorm.reshape(1, sw), bvn=b_vnorm.reshape(1, sw), bd=bd,
        wbb=w_branch_b.astype(BF16), wba=w_branch_a.astype(BF16), wo=w_out.astype(BF16),
        wrh=wrh, wrl=wrl, wsg=w_sh_gate.astype(BF16), wsu=w_sh_up.astype(BF16),
        wsd=w_sh_down.astype(BF16))


def _spatial_weights(w_spatial, b_spatial, rows_are_sequences, tl):
    if rows_are_sequences:
        wsp = w_spatial[:, 0, 0][:, None, None] * jnp.eye(CHUNK, dtype=F32)[None]
        b = jnp.broadcast_to(b_spatial[:, 0:1], (SGU_GROUPS, CHUNK))
    else:
        wsp = jnp.where(jnp.tril(jnp.ones((CHUNK, CHUNK), bool)), w_spatial, 0)
        b = b_spatial
    half = LANES // 2
    bsp = jnp.repeat(b.reshape(SGU_GROUPS // 2, 2, CHUNK), half, axis=1)
    bsp = bsp.transpose(0, 2, 1)
    r = jnp.arange(tl)
    ltri = (r[:, None] >= r[None, :]).astype(BF16)
    return dict(wsp=wsp.astype(BF16), bsp=bsp, ltri=ltri)


def _layer(x, mod, attend, wts, w_spatial, b_spatial, b_router, w_exp_gate, w_exp_up, w_exp_down,
           rows_are_sequences, tl, tm, tt_route, tt_disp, tt_comb):
    bx, l, d = x.shape
    wts = dict(wts, **_spatial_weights(w_spatial, b_spatial, rows_are_sequences, tl))
    q, kf, vf, kb, vb, lf, fc, mb, sga, vn = _mix_in(x, mod, wts, tl)
    oa = attend(q, kf, vf, kb, vb, lf, fc)
    h2, logits, base = _mix_out(x, oa, sga, mb, mod, wts, tl)
    t = bx * l
    idx_t, w_t, c_t, counts = _route(logits.reshape(t, LANES).T, b_router, tt_route)
    pos, plan = _dispatch_plan(idx_t, c_t, counts, tm)
    xs = _dispatch(pos, h2.reshape(t, d // 2), tt_disp)
    eo = _moe(plan, xs, w_exp_gate, w_exp_up, w_exp_down, tm)
    y = _combine(pos, eo, w_t.T.reshape(bx, l, TOP_K), base, mod, tt_comb)
    return y, kf, vf, lf, vn


def kernel(x_prompt, x_sample, c_prompt, c_sample, cache_k, cache_v, cache_logf, page_table, w_ada, b_ada, g_norm_mix, g_norm_ffn, w_in, b_forget, g_q, g_k, g_vnorm, b_vnorm, w_spatial, b_spatial, w_branch_a, w_branch_b, w_out, w_router, b_router, w_exp_gate, w_exp_up, w_exp_down, w_sh_gate, w_sh_up, w_sh_down):
    assert w_ada.shape[0] == 1, "one layer"
    b, s, d = x_prompt.shape
    nb = x_sample.shape[0]

    c_all = jnp.concatenate([c_prompt, c_sample], axis=0)
    pad = (-c_all.shape[0]) % 8
    c_all = jnp.pad(c_all, ((0, pad), (0, 0)))
    mod_all = _ada(c_all, w_ada[0], b_ada[0])
    mod_p = mod_all[:b].reshape(b, 1, 6 * d)
    mod_s = mod_all[b:b + nb].reshape(1, nb, 6 * d)

    wts = _prep_weights(w_in[0], b_forget[0], g_q[0], g_k[0], g_vnorm[0], b_vnorm[0], g_norm_mix[0],
                        g_norm_ffn[0], w_branch_a[0], w_branch_b[0], w_out[0], w_router[0],
                        w_sh_gate[0], w_sh_up[0], w_sh_down[0])
    experts = (w_exp_gate[0], w_exp_up[0], w_exp_down[0])

    tq = min(512, s)

    def attend_prompt(q, kf, vf, kb, vb, lf, fc):
        return _attn_prompt(q, kb, vb, fc.transpose(0, 2, 1), tq)

    def attend_sample(q, kf, vf, kb, vb, lf, fc):
        def lane_rep(a):
            a = a.astype(F32).reshape(nb, N_HEADS, HEAD_DIM, 1)
            return jnp.broadcast_to(a, (nb, N_HEADS, HEAD_DIM, PAGE))

        o = _attn_decode(page_table, lane_rep(q), lane_rep(kf), lane_rep(vf),
                         lf.reshape(nb, N_HEADS, 1),
                         cache_logf.transpose(0, 1, 3, 2),
                         cache_k.transpose(0, 1, 3, 4, 2),
                         cache_v.transpose(0, 1, 3, 4, 2))
        return o[..., 0].reshape(1, nb, ATTN_WIDTH).astype(BF16)

    y_s, k_s, v_s, lf_s, vn_s = _layer(x_sample.reshape(1, nb, d), mod_s, attend_sample, wts,
                                       w_spatial[0], b_spatial[0], b_router[0], *experts,
                                       rows_are_sequences=True, tl=nb, tm=32, tt_route=nb,
                                       tt_disp=nb, tt_comb=64)
    y_p, k_p, v_p, lf_p, _ = _layer(x_prompt, mod_p, attend_prompt, wts, w_spatial[0], b_spatial[0],
                                    b_router[0], *experts, rows_are_sequences=False,
                                    tl=min(512, s), tm=min(512, s), tt_route=min(1024, b * s),
                                    tt_disp=min(1024, s), tt_comb=256)
    hd5 = (1, b, s, N_HEADS, HEAD_DIM)
    sd5 = (1, nb, 1, N_HEADS, HEAD_DIM)
    return (y_p, y_s.reshape(nb, 1, d),
            k_p.reshape(hd5), v_p.reshape(hd5), lf_p.reshape(1, b, s, N_HEADS),
            k_s.reshape(sd5), v_s.reshape(sd5), lf_s.reshape(1, nb, 1, N_HEADS),
            vn_s.reshape(1, nb, 1, SGU_WIDTH))
```

```python
import functools

import jax
import jax.numpy as jnp
from jax import lax
from jax.experimental import pallas as pl
from jax.experimental.pallas import tpu as pltpu

F32 = jnp.float32
BF16 = jnp.bfloat16
I32 = jnp.int32
U32 = jnp.uint32

D_MODEL = 1024
N_HEADS = 8
HEAD_DIM = 64
ATTN_WIDTH = N_HEADS * HEAD_DIM
SGU_GROUPS = 8
SGU_WIDTH = 512
CHUNK = 128
N_EXPERTS = 64
TOP_K = 8
N_EXPERT_GROUPS = 8
TOP_K_GROUPS = 4
EXPERTS_PER_GROUP = 8
D_EXPERT = 256
D_SHARED = 256
ROUTED_SCALE = 2.5
NORM_EPS = 1e-6
ATTN_SCALE = HEAD_DIM ** -0.5
LOG2E = 1.4426950408889634
PAGE = 128
LANES = 128
VMEM_LIMIT = 56 * 1024 * 1024
PAGES_PER_STEP = 16
SEG_ALIGN = 8
SORT_CHUNK = 512
RUN_BLOCKS = (256, 128, 64, 32, 16, 8)
REGROUP_STEPS = 8

_dot = functools.partial(jnp.dot, preferred_element_type=F32)


def _dot_nt(a, b):
    return lax.dot_general(a, b, (((1,), (1,)), ((), ())), preferred_element_type=F32)


def _sigmoid(x):
    return 1.0 / (1.0 + jnp.exp(-x))


def _gelu(x):
    return 0.5 * x * (1.0 + jnp.tanh(0.7978845608028654 * (x + 0.044715 * (x * x * x))))


def _split3(x):
    hi = x.astype(BF16)
    r1 = x - hi.astype(F32)
    mid = r1.astype(BF16)
    lo = (r1 - mid.astype(F32)).astype(BF16)
    return hi, mid, lo


def _pack_halves(x):
    w = x.shape[1] // 2
    lo = lax.bitcast_convert_type(x[:, :w].astype(BF16).astype(F32), U32)
    hi = lax.bitcast_convert_type(x[:, w:].astype(BF16).astype(F32), U32)
    return (hi & jnp.uint32(0xFFFF0000)) | (lo >> 16)


def _unpack_halves(p):
    return (lax.bitcast_convert_type(p << 16, F32),
            lax.bitcast_convert_type(p & jnp.uint32(0xFFFF0000), F32))


def _dot3_left(m_bf16, x):
    hi, mid, lo = _split3(x)
    return _dot(m_bf16, hi) + _dot(m_bf16, mid) + _dot(m_bf16, lo)


def _dot3_right(x, m_bf16):
    hi, mid, lo = _split3(x)
    return _dot(hi, m_bf16) + _dot(mid, m_bf16) + _dot(lo, m_bf16)


def _params(*sem, row_dma=False):
    return pltpu.CompilerParams(dimension_semantics=sem, vmem_limit_bytes=VMEM_LIMIT,
                                disable_bounds_checks=row_dma)


def _const_spec(shape):
    zeros = (0,) * len(shape)
    return pl.BlockSpec(shape, lambda *_: zeros)


def _ada_kernel(c_ref, w_ref, b_ref, o_ref):
    c = c_ref[...]
    a = c * _sigmoid(c)
    o_ref[...] = jnp.dot(a, w_ref[...], preferred_element_type=F32,
                         precision=lax.Precision.HIGHEST) + b_ref[...]


def _ada(c, w_ada, b_ada):
    m, d = c.shape
    n = w_ada.shape[1]
    tn = 1024
    return pl.pallas_call(
        _ada_kernel,
        grid=(n // tn,),
        in_specs=[pl.BlockSpec((m, d), lambda j: (0, 0)),
                  pl.BlockSpec((d, tn), lambda j: (0, j)),
                  pl.BlockSpec((1, tn), lambda j: (0, j))],
        out_specs=pl.BlockSpec((m, tn), lambda j: (0, j)),
        out_shape=jax.ShapeDtypeStruct((m, n), F32),
        compiler_params=_params("arbitrary"),
        name="ada",
    )(c, w_ada, b_ada.reshape(1, n))


def _mix_in_kernel(x_ref, mod_ref, gmix_ref, wq_ref, wk_ref, wv_ref, wf_ref, wu_ref, wvg_ref,
                   wga_ref, wgb_ref, bf_ref, gq_ref, gk_ref, gvn_ref, bvn_ref, bd_ref,
                   wsp_ref, bsp_ref, wbb_ref, ltri_ref,
                   q_ref, kf_ref, vf_ref, kb_ref, vb_ref, lf_ref, fc_ref, mb_ref, sga_ref, vn_ref,
                   carry_sc):
    tl = x_ref.shape[1]
    x = x_ref[0]
    shift1 = mod_ref[0, :, 0:D_MODEL]
    scale1 = mod_ref[0, :, D_MODEL:2 * D_MODEL]
    ms = jnp.mean(x * x, axis=-1, keepdims=True)
    h = x * lax.rsqrt(ms + NORM_EPS) * gmix_ref[...] * (1.0 + scale1) + shift1
    hb = h.astype(BF16)
    bd = bd_ref[...]

    def head_norm(z, g):
        ss = _dot((z * z).astype(BF16), bd) * (1.0 / HEAD_DIM)
        return z * lax.rsqrt(ss + NORM_EPS) * g

    qn = head_norm(_dot(hb, wq_ref[...]), gq_ref[...]) * (ATTN_SCALE * LOG2E)
    q_ref[0] = qn.astype(BF16)
    kn = head_norm(_dot(hb, wk_ref[...]), gk_ref[...])
    kf_ref[0] = kn
    kb_ref[0] = kn.astype(BF16)
    v = _dot(hb, wv_ref[...])
    vf_ref[0] = v
    vb_ref[0] = v.astype(BF16)

    zf = _dot(hb, wf_ref[...]) + bf_ref[...]
    lf = jnp.minimum(zf, 0.0) - jnp.log(1.0 + jnp.exp(-jnp.abs(zf)))
    lf_ref[0] = lf[:, :N_HEADS]

    @pl.when(pl.program_id(1) == 0)
    def _():
        carry_sc[...] = jnp.zeros_like(carry_sc)

    fc = _dot3_left(ltri_ref[...], lf) + carry_sc[...]
    fc_ref[0] = fc[:, :N_HEADS]
    carry_sc[...] = fc[tl - 1:tl, :]

    gu = _gelu(_dot(hb, wu_ref[...]))
    gv = _gelu(_dot(hb, wvg_ref[...]))
    mu = jnp.mean(gv, axis=-1, keepdims=True)
    gc = gv - mu
    var = jnp.mean(gc * gc, axis=-1, keepdims=True)
    vn = gc * lax.rsqrt(var + NORM_EPS) * gvn_ref[...] + bvn_ref[...]
    vn_ref[0] = vn
    vnb = vn.astype(BF16)
    lane = lax.broadcasted_iota(I32, (CHUNK, LANES), 1)
    low = lane < (LANES // 2)
    zero = jnp.zeros((CHUNK, LANES), BF16)
    rows = []
    for c in range(tl // CHUNK):
        pieces = []
        for j in range(SGU_WIDTH // LANES):
            vp = vnb[c * CHUNK:(c + 1) * CHUNK, j * LANES:(j + 1) * LANES]
            mixed = (_dot(wsp_ref[2 * j], jnp.where(low, vp, zero))
                     + _dot(wsp_ref[2 * j + 1], jnp.where(low, zero, vp)) + bsp_ref[j])
            pieces.append(mixed)
        rows.append(jnp.concatenate(pieces, axis=1))
    mixed = rows[0] if len(rows) == 1 else jnp.concatenate(rows, axis=0)
    ob = (gu * mixed).astype(BF16)
    mb = _sigmoid(_dot(hb, wgb_ref[...])) * _dot(ob, wbb_ref[...])
    mb_ref[0] = mb.astype(BF16)
    sga_ref[0] = _sigmoid(_dot(hb, wga_ref[...])).astype(BF16)


def _mix_in(x, mod, wts, tl):
    bx, l, d = x.shape
    tlm = tl if mod.shape[1] > 1 else 1
    grid = (bx, l // tl)
    row = lambda b, i: (b, i, 0)
    mod_map = row if tlm > 1 else (lambda b, i: (b, 0, 0))
    names = ("gmix", "wq", "wk", "wv", "wf", "wu", "wvg", "wga", "wgb", "bf", "gq", "gk", "gvn",
             "bvn", "bd", "wsp", "bsp", "wbb", "ltri")
    consts = [wts[n] for n in names]
    out_widths = [(ATTN_WIDTH, BF16), (ATTN_WIDTH, F32), (ATTN_WIDTH, F32), (ATTN_WIDTH, BF16),
                  (ATTN_WIDTH, BF16), (N_HEADS, F32), (N_HEADS, F32), (D_MODEL, BF16),
                  (D_MODEL, BF16), (SGU_WIDTH, F32)]
    return pl.pallas_call(
        _mix_in_kernel,
        grid=grid,
        in_specs=[pl.BlockSpec((1, tl, d), row), pl.BlockSpec((1, tlm, 6 * d), mod_map)]
                 + [_const_spec(c.shape) for c in consts],
        out_specs=[pl.BlockSpec((1, tl, w), row) for w, _ in out_widths],
        out_shape=[jax.ShapeDtypeStruct((bx, l, w), dt) for w, dt in out_widths],
        scratch_shapes=[pltpu.VMEM((1, LANES), F32)],
        compiler_params=_params("arbitrary", "arbitrary"),
        name="mix_in",
    )(x, mod, *consts)


def _attn_kernel(qi_ref, ki_ref, q_ref, k_ref, v_ref, fq_ref, fk_ref, o_ref, m_sc, l_sc, acc_sc):
    tq = q_ref.shape[1]
    tk = k_ref.shape[1]
    pair = pl.program_id(1)
    qi = qi_ref[pair]
    ki = ki_ref[pair]

    @pl.when(ki == 0)
    def _():
        m_sc[...] = jnp.full_like(m_sc, -jnp.inf)
        l_sc[...] = jnp.zeros_like(l_sc)
        acc_sc[...] = jnp.zeros_like(acc_sc)

    lane = lax.broadcasted_iota(I32, (tq, LANES), 1)
    low = lane < HEAD_DIM

    def step(masked):
        ones = jnp.ones((tk, LANES), BF16)
        if masked:
            causal = (lax.broadcasted_iota(I32, (tq, tk), 0) >= lax.broadcasted_iota(I32, (tq, tk), 1))
        for j in range(ATTN_WIDTH // LANES):
            sl = slice(j * LANES, (j + 1) * LANES)
            qp = q_ref[0, :, sl]
            kp = k_ref[0, :, sl]
            vx = jnp.concatenate([v_ref[0, :, sl], ones], axis=1)
            zero = jnp.zeros_like(qp)
            alphas = []
            pvs = []
            for t in range(2):
                hd = 2 * j + t
                qh = jnp.where(low, qp, zero) if t == 0 else jnp.where(low, zero, qp)
                decay = (fq_ref[0, hd:hd + 1, 0:1] - fk_ref[0, hd:hd + 1, :]) * LOG2E
                s = _dot_nt(qh, kp) + decay
                if masked:
                    s = jnp.where(causal, s, -jnp.inf)
                m_prev = m_sc[hd]
                m_new = jnp.maximum(m_prev, jnp.max(s, axis=-1, keepdims=True))
                alpha = jnp.exp2(m_prev - m_new)
                p = jnp.concatenate([jnp.exp2(s[:, c * LANES:(c + 1) * LANES] - m_new)
                                     for c in range(tk // LANES)], axis=1)
                pv = _dot(p.astype(BF16), vx)
                l_sc[hd] = alpha * l_sc[hd] + pv[:, LANES:]
                m_sc[hd] = m_new
                alphas.append(alpha)
                pvs.append(pv[:, :LANES])
            acc_sc[j] = (acc_sc[j] * jnp.where(low, alphas[0], alphas[1])
                         + jnp.where(low, pvs[0], pvs[1]))

    @pl.when(ki < qi)
    def _():
        step(False)

    @pl.when(ki == qi)
    def _():
        step(True)
        for j in range(ATTN_WIDTH // LANES):
            inv = jnp.where(low, 1.0 / l_sc[2 * j], 1.0 / l_sc[2 * j + 1])
            o_ref[0, :, j * LANES:(j + 1) * LANES] = (acc_sc[j] * inv).astype(o_ref.dtype)


def _attn_prompt(q, k, v, fr, tq):
    b, s, w = q.shape
    nq = s // tq
    pairs = [(qi, ki) for qi in range(nq) for ki in range(qi + 1)]
    qi_arr = jnp.asarray([p[0] for p in pairs], I32)
    ki_arr = jnp.asarray([p[1] for p in pairs], I32)
    qmap = lambda bi, p, qa, ka: (bi, qa[p], 0)
    kmap = lambda bi, p, qa, ka: (bi, ka[p], 0)
    grid_spec = pltpu.PrefetchScalarGridSpec(
        num_scalar_prefetch=2,
        grid=(b, len(pairs)),
        in_specs=[pl.BlockSpec((1, tq, w), qmap),
                  pl.BlockSpec((1, tq, w), kmap),
                  pl.BlockSpec((1, tq, w), kmap),
                  pl.BlockSpec((1, N_HEADS, tq), lambda bi, p, qa, ka: (bi, 0, qa[p])),
                  pl.BlockSpec((1, N_HEADS, tq), lambda bi, p, qa, ka: (bi, 0, ka[p]))],
        out_specs=pl.BlockSpec((1, tq, w), qmap),
        scratch_shapes=[pltpu.VMEM((N_HEADS, tq, LANES), F32), pltpu.VMEM((N_HEADS, tq, LANES), F32),
                        pltpu.VMEM((w // LANES, tq, LANES), F32)],
    )
    return pl.pallas_call(
        _attn_kernel,
        grid_spec=grid_spec,
        out_shape=jax.ShapeDtypeStruct((b, s, w), BF16),
        compiler_params=_params("arbitrary", "arbitrary"),
        name="attn_prompt",
    )(qi_arr, ki_arr, q, k, v, fr, fr)


def _rows_to_tile(rows):
    sub = lax.broadcasted_iota(I32, (N_HEADS, PAGE), 0)
    tile = jnp.zeros((N_HEADS, PAGE), F32)
    for h, r in enumerate(rows):
        tile = jnp.where(sub == h, r, tile)
    return tile


def _attn_decode_kernel(pt_ref, q_ref, kn_ref, vn_ref, lfn_ref, u_ref, *refs):
    n = PAGES_PER_STEP
    lf_refs, k_refs, v_refs = refs[:n], refs[n:2 * n], refs[2 * n:3 * n]
    o_ref, m_sc, l_sc, acc_sc, carry_sc = refs[3 * n:]
    j = pl.program_id(1)
    nj = pl.num_programs(1)
    heads = range(N_HEADS)

    @pl.when(j == 0)
    def _():
        m_sc[...] = jnp.full_like(m_sc, -jnp.inf)
        l_sc[...] = jnp.zeros_like(l_sc)
        acc_sc[...] = jnp.zeros_like(acc_sc)
        carry_sc[...] = jnp.zeros_like(carry_sc)

    lfn = lfn_ref[0]
    lf_all = jnp.concatenate([lf_refs[i][0, 0] for i in range(n)], axis=0)
    later_all = _dot3_right(lf_all, u_ref[...])
    carry = carry_sc[...]
    decays = []
    for i in range(n):
        sl = slice(i * N_HEADS, (i + 1) * N_HEADS)
        decays.append((later_all[sl] + carry + lfn) * LOG2E)
        carry = carry + jnp.sum(lf_all[sl], axis=-1, keepdims=True)
    carry_sc[...] = carry

    rows = [[None] * N_HEADS for _ in range(n)]
    for h in heads:
        qh = q_ref[0, h]
        for i in range(n):
            rows[i][h] = jnp.sum(k_refs[i][0, 0, h] * qh, axis=0, keepdims=True)
    s = [_rows_to_tile(rows[i]) + decays[i] for i in range(n)]
    m = m_sc[...]
    m_new = m
    for i in range(n):
        m_new = jnp.maximum(m_new, jnp.max(s[i], axis=-1, keepdims=True))
    alpha = jnp.exp2(m - m_new)
    p = [jnp.exp2(s[i] - m_new) for i in range(n)]
    l = alpha * l_sc[...]
    for i in range(n):
        l = l + jnp.sum(p[i], axis=-1, keepdims=True)
    m_sc[...] = m_new
    l_sc[...] = l
    alpha_rep = jnp.broadcast_to(alpha, (N_HEADS, PAGE))
    for h in heads:
        a = acc_sc[h] * alpha_rep[h:h + 1, :]
        for i in range(n):
            a = a + p[i][h:h + 1, :] * v_refs[i][0, 0, h]
        acc_sc[h] = a

    @pl.when(j == nj - 1)
    def _():
        s_n = _rows_to_tile([jnp.sum(q_ref[0, h] * kn_ref[0, h], axis=0, keepdims=True) for h in heads])
        m_rep = jnp.broadcast_to(m_new, (N_HEADS, PAGE))
        m_fin = jnp.maximum(m_rep, s_n)
        a_fin = jnp.exp2(m_rep - m_fin)
        p_n = jnp.exp2(s_n - m_fin)
        inv = 1.0 / (a_fin * jnp.broadcast_to(l, (N_HEADS, PAGE)) + p_n)
        for h in heads:
            total = jnp.broadcast_to(jnp.sum(acc_sc[h], axis=-1, keepdims=True), (HEAD_DIM, PAGE))
            o_ref[0, h] = (a_fin[h:h + 1, :] * total + p_n[h:h + 1, :] * vn_ref[0, h]) * inv[h:h + 1, :]


def _attn_decode(page_table, q_rep, k_new_rep, v_new_rep, lf_new, cache_lf_t, cache_k_t, cache_v_t):
    nb, n_pages = page_table.shape
    n = PAGES_PER_STEP
    tok3 = lambda b, j, pt: (b, 0, 0)
    tok4 = lambda b, j, pt: (b, 0, 0, 0)

    def page(i, rank):
        def index_map(b, j, pt):
            return (0, pt[b * n_pages + n_pages - 1 - (j * n + i)]) + (0,) * (rank - 2)
        return index_map

    lane = jnp.arange(PAGE)
    later = (lane[:, None] > lane[None, :]).astype(BF16)
    rep = pl.BlockSpec((1, N_HEADS, HEAD_DIM, PAGE), tok4)
    grid_spec = pltpu.PrefetchScalarGridSpec(
        num_scalar_prefetch=1,
        grid=(nb, n_pages // n),
        in_specs=[rep, rep, rep,
                  pl.BlockSpec((1, N_HEADS, 1), tok3),
                  pl.BlockSpec((PAGE, PAGE), lambda b, j, pt: (0, 0))]
                 + [pl.BlockSpec((1, 1, N_HEADS, PAGE), page(i, 4)) for i in range(n)]
                 + [pl.BlockSpec((1, 1, N_HEADS, HEAD_DIM, PAGE), page(i, 5)) for i in range(n)]
                 + [pl.BlockSpec((1, 1, N_HEADS, HEAD_DIM, PAGE), page(i, 5)) for i in range(n)],
        out_specs=rep,
        scratch_shapes=[pltpu.VMEM((N_HEADS, 1), F32), pltpu.VMEM((N_HEADS, 1), F32),
                        pltpu.VMEM((N_HEADS, HEAD_DIM, PAGE), F32), pltpu.VMEM((N_HEADS, 1), F32)],
    )
    return pl.pallas_call(
        _attn_decode_kernel,
        grid_spec=grid_spec,
        out_shape=jax.ShapeDtypeStruct((nb, N_HEADS, HEAD_DIM, PAGE), F32),
        compiler_params=_params("arbitrary", "arbitrary"),
        name="attn_decode",
    )(page_table.reshape(-1), q_rep, k_new_rep, v_new_rep, lf_new, later,
      *([cache_lf_t] * n), *([cache_k_t] * n), *([cache_v_t] * n))


def _mix_out_kernel(x_ref, oa_ref, sga_ref, mb_ref, mod_ref, wba_ref, wo_ref, gffn_ref,
                    wrh_ref, wrl_ref, wsg_ref, wsu_ref, wsd_ref,
                    h2_ref, lg_ref, base_ref):
    d = D_MODEL
    x = x_ref[0]
    gate1 = mod_ref[0, :, 2 * d:3 * d]
    shift2 = mod_ref[0, :, 3 * d:4 * d]
    scale2 = mod_ref[0, :, 4 * d:5 * d]
    gate2 = mod_ref[0, :, 5 * d:6 * d]
    merged = sga_ref[0].astype(F32) * _dot(oa_ref[0], wba_ref[...]) + mb_ref[0].astype(F32)
    x1 = x + gate1 * _dot(merged.astype(BF16), wo_ref[...])
    ms = jnp.mean(x1 * x1, axis=-1, keepdims=True)
    h2 = x1 * lax.rsqrt(ms + NORM_EPS) * gffn_ref[...] * (1.0 + scale2) + shift2
    h2_ref[0] = _pack_halves(h2)
    hb = h2.astype(BF16)
    hl = (h2 - hb.astype(F32)).astype(BF16)
    lg_ref[0] = _dot(hb, wrh_ref[...]) + (_dot(hb, wrl_ref[...]) + _dot(hl, wrh_ref[...]))
    g = _dot(hb, wsg_ref[...])
    u = _dot(hb, wsu_ref[...])
    a = (g * _sigmoid(g) * u).astype(BF16)
    base_ref[0] = x1 + gate2 * _dot(a, wsd_ref[...])


def _mix_out(x, oa, sga, mb, mod, wts, tl):
    bx, l, d = x.shape
    tlm = tl if mod.shape[1] > 1 else 1
    row = lambda b, i: (b, i, 0)
    mod_map = row if tlm > 1 else (lambda b, i: (b, 0, 0))
    names = ("wba", "wo", "gffn", "wrh", "wrl", "wsg", "wsu", "wsd")
    consts = [wts[n] for n in names]
    return pl.pallas_call(
        _mix_out_kernel,
        grid=(bx, l // tl),
        in_specs=[pl.BlockSpec((1, tl, d), row), pl.BlockSpec((1, tl, ATTN_WIDTH), row),
                  pl.BlockSpec((1, tl, d), row), pl.BlockSpec((1, tl, d), row),
                  pl.BlockSpec((1, tlm, 6 * d), mod_map)] + [_const_spec(c.shape) for c in consts],
        out_specs=[pl.BlockSpec((1, tl, d // 2), row), pl.BlockSpec((1, tl, LANES), row),
                   pl.BlockSpec((1, tl, d), row)],
        out_shape=[jax.ShapeDtypeStruct((bx, l, d // 2), U32), jax.ShapeDtypeStruct((bx, l, LANES), F32),
                   jax.ShapeDtypeStruct((bx, l, d), F32)],
        compiler_params=_params("arbitrary", "arbitrary"),
        name="mix_out",
    )(x, oa, sga, mb, mod, *consts)


def _route_kernel(lg_ref, b_ref, before_ref, lower_ref, w_ref, r_ref, cnt_ref):
    tt = lg_ref.shape[1]
    epg = EXPERTS_PER_GROUP
    ninf = -jnp.inf
    iota = lax.broadcasted_iota(I32, (epg, tt), 0)
    sc = []
    biased = []
    gscore = []
    for g in range(N_EXPERT_GROUPS):
        s = _sigmoid(lg_ref[g * epg:(g + 1) * epg, :])
        bz = s + b_ref[g * epg:(g + 1) * epg, :]
        m1 = jnp.max(bz, axis=0, keepdims=True)
        first = jnp.min(jnp.where(bz == m1, iota, epg), axis=0, keepdims=True)
        m2 = jnp.max(jnp.where(iota == first, ninf, bz), axis=0, keepdims=True)
        sc.append(s)
        biased.append(bz)
        gscore.append(m1 + m2)
    cand = []
    for g in range(N_EXPERT_GROUPS):
        rank = jnp.zeros((1, tt), I32)
        for o in range(N_EXPERT_GROUPS):
            if o == g:
                continue
            beats = (gscore[o] >= gscore[g]) if o < g else (gscore[o] > gscore[g])
            rank = rank + beats.astype(I32)
        cand.append(jnp.where(rank < TOP_K_GROUPS, biased[g], ninf))
    ws = []
    picks = []
    for k in range(TOP_K):
        mx = cand[0]
        for g in range(1, N_EXPERT_GROUPS):
            mx = jnp.maximum(mx, cand[g])
        mx = jnp.max(mx, axis=0, keepdims=True)
        fi = jnp.where(cand[0] == mx, iota, N_EXPERTS)
        for g in range(1, N_EXPERT_GROUPS):
            fi = jnp.minimum(fi, jnp.where(cand[g] == mx, iota + g * epg, N_EXPERTS))
        fi = jnp.min(fi, axis=0, keepdims=True)
        wk = jnp.zeros((epg, tt), F32)
        for g in range(N_EXPERT_GROUPS):
            hit = (iota + g * epg) == fi
            wk = wk + jnp.where(hit, sc[g], 0.0)
            cand[g] = jnp.where(hit, ninf, cand[g])
        picks.append(fi)
        ws.append(jnp.sum(wk, axis=0, keepdims=True))
    tot = ws[0]
    for k in range(1, TOP_K):
        tot = tot + ws[k]
    for k in range(TOP_K):
        w_ref[k:k + 1, :] = ws[k] / tot * ROUTED_SCALE

    chosen = []
    for g in range(N_EXPERT_GROUPS):
        sel = jnp.zeros((epg, tt), F32)
        for k in range(TOP_K):
            sel = sel + jnp.where((iota + g * epg) == picks[k], 1.0, 0.0)
        chosen.append(sel)
    chosen = jnp.concatenate(chosen, axis=0)
    earlier = _dot(chosen.astype(BF16), before_ref[...])
    run = jnp.floor((jnp.sum(chosen, axis=1, keepdims=True) + (SEG_ALIGN - 1)) * (1.0 / SEG_ALIGN)) * SEG_ALIGN
    run_rep = jnp.broadcast_to(run, (N_EXPERTS, LANES))
    cnt_ref[0] = run_rep
    start = _dot(lower_ref[...], run_rep.astype(BF16))[:, 0:1]
    local = earlier + start
    for k in range(TOP_K):
        ck = jnp.zeros((epg, tt), F32)
        for g in range(N_EXPERT_GROUPS):
            ck = ck + jnp.where((iota + g * epg) == picks[k], local[g * epg:(g + 1) * epg, :], 0.0)
        r_ref[k:k + 1, :] = jnp.sum(ck, axis=0, keepdims=True).astype(I32)


def _route(logits_t, b_router, tt):
    t = logits_t.shape[1]
    r = jnp.arange(tt)
    before = (r[:, None] < r[None, :]).astype(BF16)
    e = jnp.arange(N_EXPERTS)
    lower = (e[None, :] < e[:, None]).astype(BF16)
    col = lambda i: (0, i)
    return pl.pallas_call(
        _route_kernel,
        grid=(t // tt,),
        in_specs=[pl.BlockSpec((N_EXPERTS, tt), col),
                  pl.BlockSpec((N_EXPERTS, 1), lambda i: (0, 0)),
                  pl.BlockSpec((tt, tt), lambda i: (0, 0)),
                  pl.BlockSpec((N_EXPERTS, N_EXPERTS), lambda i: (0, 0))],
        out_specs=[pl.BlockSpec((TOP_K, tt), col), pl.BlockSpec((TOP_K, tt), col),
                   pl.BlockSpec((1, N_EXPERTS, LANES), lambda i: (i, 0, 0))],
        out_shape=[jax.ShapeDtypeStruct((TOP_K, t), F32), jax.ShapeDtypeStruct((TOP_K, t), I32),
                   jax.ShapeDtypeStruct((t // tt, N_EXPERTS, LANES), F32)],
        compiler_params=_params("arbitrary"),
        name="route",
    )(logits_t, b_router.reshape(N_EXPERTS, 1), before, lower)


def _local_sort_kernel(r_ref, h_ref, o_ref):
    tt = h_ref.shape[0]
    h = jnp.concatenate(_unpack_halves(h_ref[...]), axis=1).astype(BF16)
    for c in range(o_ref.shape[0] // SORT_CHUNK):
        rows = c * SORT_CHUNK + lax.broadcasted_iota(I32, (SORT_CHUNK, tt), 0)
        hit = rows == r_ref[0:1, :]
        for k in range(1, TOP_K):
            hit = hit | (rows == r_ref[k:k + 1, :])
        place = jnp.where(hit, 1.0, 0.0).astype(BF16)
        o_ref[c * SORT_CHUNK:(c + 1) * SORT_CHUNK, :] = _pack_halves(_dot(place, h))


def _local_rows(tt):
    pad = N_EXPERTS * SEG_ALIGN
    return -(-(tt * TOP_K + pad) // SORT_CHUNK) * SORT_CHUNK


def _local_sort(r_t, h2, tt):
    t, dp = h2.shape
    rows = _local_rows(tt)
    return pl.pallas_call(
        _local_sort_kernel,
        grid=(t // tt,),
        in_specs=[pl.BlockSpec((TOP_K, tt), lambda i: (0, i)), pl.BlockSpec((tt, dp), lambda i: (i, 0))],
        out_specs=pl.BlockSpec((rows, dp), lambda i: (i, 0)),
        out_shape=jax.ShapeDtypeStruct((t // tt * rows, dp), U32),
        compiler_params=_params("arbitrary"),
        name="moe_local_sort",
    )(r_t, h2)


def _regroup_kernel(src_ref, dst_ref, len_ref, a_hbm, *rest):
    o_hbm, sem = rest[-2:]
    per = len_ref.shape[0] // pl.num_programs(0)
    first = pl.program_id(0) * per

    def for_each_block(i, fn):
        s0, d0, n = src_ref[i], dst_ref[i], len_ref[i]
        for size in RUN_BLOCKS:
            @pl.when((n & size) != 0)
            def _():
                off = (n // (2 * size)) * (2 * size)
                fn(pltpu.make_async_copy(a_hbm.at[pl.ds(pl.multiple_of(s0 + off, SEG_ALIGN), size)],
                                         o_hbm.at[pl.ds(pl.multiple_of(d0 + off, SEG_ALIGN), size)],
                                         sem.at[0]))

    def issue(i, c):
        for_each_block(first + i, lambda cp: cp.start())
        return c

    def wait(i, c):
        for_each_block(first + i, lambda cp: cp.wait())
        return c

    lax.fori_loop(0, per, issue, 0)
    lax.fori_loop(0, per, wait, 0)


def _regroup(src, dst, length, a, into, out_rows):
    steps = REGROUP_STEPS if length.shape[0] % REGROUP_STEPS == 0 else 1
    operands = [a] if into is None else [a, into]
    grid_spec = pltpu.PrefetchScalarGridSpec(
        num_scalar_prefetch=3,
        grid=(steps,),
        in_specs=[pl.BlockSpec(memory_space=pl.ANY)] * len(operands),
        out_specs=pl.BlockSpec(memory_space=pl.ANY),
        scratch_shapes=[pltpu.SemaphoreType.DMA((1,))],
    )
    return pl.pallas_call(
        _regroup_kernel,
        grid_spec=grid_spec,
        out_shape=jax.ShapeDtypeStruct((out_rows, a.shape[1]), a.dtype),
        input_output_aliases={} if into is None else {4: 0},
        compiler_params=_params("arbitrary", row_dma=True),
        name="moe_regroup",
    )(src, dst, length, *operands)


VISIT_ACTIVE = 1
VISIT_FIRST = 2


def _moe_kernel(vt_ref, ve_ref, vf_ref, gs_ref, ge_ref, x_ref, wg_ref, wu_ref, wd_ref, o_ref):
    v = pl.program_id(0)
    tm = x_ref.shape[0]
    flags = vf_ref[v]

    @pl.when(flags >= VISIT_ACTIVE)
    def _():
        e = ve_ref[v]
        x = jnp.concatenate(_unpack_halves(x_ref[...]), axis=1).astype(BF16)
        g = _dot(x, wg_ref[0].astype(BF16))
        u = _dot(x, wu_ref[0].astype(BF16))
        a = (g * _sigmoid(g) * u).astype(BF16)
        res = _pack_halves(_dot(a, wd_ref[0].astype(BF16)))
        row = vt_ref[v] * tm + lax.broadcasted_iota(I32, (tm, 1), 0)
        mine = (row >= gs_ref[e]) & (row < ge_ref[e])

        @pl.when(flags >= VISIT_FIRST)
        def _():
            o_ref[...] = jnp.where(mine, res, jnp.zeros_like(res))

        @pl.when(flags < VISIT_FIRST)
        def _():
            o_ref[...] = jnp.where(mine, res, o_ref[...])


def _moe(plan, xs, w_gate, w_up, w_down, tm):
    visit_tile, visit_e, visit_flags, gs, ge = plan
    n_visits = visit_tile.shape[0]
    rows, dp = xs.shape
    d = 2 * dp
    tile = lambda v, vt, ve, vf, s, e: (vt[v], 0)
    expert = lambda v, vt, ve, vf, s, e: (ve[v], 0, 0)
    grid_spec = pltpu.PrefetchScalarGridSpec(
        num_scalar_prefetch=5,
        grid=(n_visits,),
        in_specs=[pl.BlockSpec((tm, dp), tile),
                  pl.BlockSpec((1, d, D_EXPERT), expert),
                  pl.BlockSpec((1, d, D_EXPERT), expert),
                  pl.BlockSpec((1, D_EXPERT, d), expert)],
        out_specs=pl.BlockSpec((tm, dp), tile),
    )
    return pl.pallas_call(
        _moe_kernel,
        grid_spec=grid_spec,
        out_shape=jax.ShapeDtypeStruct((rows, dp), U32),
        compiler_params=_params("arbitrary"),
        name="moe_experts",
    )(visit_tile, visit_e, visit_flags, gs, ge, xs, w_gate, w_up, w_down)


def _local_combine_kernel(r_ref, w_ref, e_ref, base_ref, mod_ref, y_ref):
    tt = base_ref.shape[1]
    half = D_MODEL // 2
    lo = jnp.zeros((tt, half), F32)
    hi = jnp.zeros((tt, half), F32)
    for c in range(e_ref.shape[0] // SORT_CHUNK):
        cols = c * SORT_CHUNK + lax.broadcasted_iota(I32, (tt, SORT_CHUNK), 1)
        wm = jnp.zeros((tt, SORT_CHUNK), F32)
        for k in range(TOP_K):
            wm = jnp.where(cols == r_ref[0, :, k:k + 1], w_ref[0, :, k:k + 1], wm)
        wh = wm.astype(BF16)
        wl = (wm - wh.astype(F32)).astype(BF16)
        e_lo, e_hi = _unpack_halves(e_ref[c * SORT_CHUNK:(c + 1) * SORT_CHUNK, :])
        e_lo, e_hi = e_lo.astype(BF16), e_hi.astype(BF16)
        lo = lo + (_dot(wh, e_lo) + _dot(wl, e_lo))
        hi = hi + (_dot(wh, e_hi) + _dot(wl, e_hi))
    gate2 = mod_ref[0, :, 5 * D_MODEL:6 * D_MODEL]
    y_ref[0, :, :half] = base_ref[0, :, :half] + gate2[:, :half] * lo
    y_ref[0, :, half:] = base_ref[0, :, half:] + gate2[:, half:] * hi


def _local_combine(r_tok, w_tok, e_tl, base, mod, tt):
    bx, l, d = base.shape
    nl = l // tt
    rows = _local_rows(tt)
    tlm = tt if mod.shape[1] > 1 else 1
    row = lambda b, i: (b, i, 0)
    mod_map = row if tlm > 1 else (lambda b, i: (b, 0, 0))
    return pl.pallas_call(
        _local_combine_kernel,
        grid=(bx, nl),
        in_specs=[pl.BlockSpec((1, tt, TOP_K), row), pl.BlockSpec((1, tt, TOP_K), row),
                  pl.BlockSpec((rows, d // 2), lambda b, i: (b * nl + i, 0)),
                  pl.BlockSpec((1, tt, d), row),
                  pl.BlockSpec((1, tlm, 6 * d), mod_map)],
        out_specs=pl.BlockSpec((1, tt, d), row),
        out_shape=jax.ShapeDtypeStruct((bx, l, d), F32),
        compiler_params=_params("arbitrary", "arbitrary"),
        name="moe_local_combine",
    )(r_tok, w_tok, e_tl, base, mod)


def _moe_plan(runs, tt, tm):
    n_tok_tiles = runs.shape[0]
    runs = runs.astype(I32)
    local_start = jnp.cumsum(runs, axis=1) - runs
    totals = jnp.sum(runs, axis=0)
    ge = jnp.cumsum(totals)
    gs = ge - totals
    tile_major = (jnp.arange(n_tok_tiles, dtype=I32) * _local_rows(tt))[:, None] + local_start
    expert_major = gs[None, :] + jnp.cumsum(runs, axis=0) - runs
    tables = (tile_major.reshape(-1), expert_major.reshape(-1), runs.reshape(-1))
    max_rows = -(-(n_tok_tiles * (tt * TOP_K + N_EXPERTS * (SEG_ALIGN - 1))) // tm) * tm
    n_tiles = max_rows // tm
    n_visits = n_tiles + N_EXPERTS
    experts = jnp.arange(N_EXPERTS, dtype=I32)
    first_tile = gs // tm
    n_vis = jnp.where(totals > 0, (ge - 1) // tm - first_tile + 1, 0)
    vend = jnp.cumsum(n_vis)
    vstart = vend - n_vis
    v = jnp.arange(n_visits, dtype=I32)
    active = v < vend[-1]
    e_of_v = jnp.minimum(jnp.sum(vend[None, :] <= v[:, None], axis=1), N_EXPERTS - 1).astype(I32)
    onehot = e_of_v[:, None] == experts[None, :]
    pick = lambda table: jnp.sum(jnp.where(onehot, table[None, :], 0), axis=1)
    last_tile = jnp.maximum(ge[-1] - 1, 0) // tm
    tile_of_v = jnp.where(active, pick(first_tile) + v - pick(vstart), last_tile).astype(I32)
    prev_tile = jnp.concatenate([jnp.full((1,), -1, I32), tile_of_v[:-1]])
    flags = jnp.where(active, VISIT_ACTIVE + VISIT_FIRST * (tile_of_v != prev_tile), 0).astype(I32)
    return tables, (tile_of_v, e_of_v, flags, gs, ge), max_rows


def _prep_weights(w_in, b_forget, g_q, g_k, g_vnorm, b_vnorm, g_norm_mix, g_norm_ffn,
                  w_branch_a, w_branch_b, w_out, w_router, w_sh_gate, w_sh_up, w_sh_down):
    aw, sw, d = ATTN_WIDTH, SGU_WIDTH, D_MODEL
    o = 3 * aw + N_HEADS
    wf = jnp.zeros((d, LANES), F32).at[:, :N_HEADS].set(w_in[:, 3 * aw:o])
    bf = jnp.zeros((1, LANES), F32).at[0, :N_HEADS].set(b_forget)
    lane = jnp.arange(aw)
    bd = (lane[:, None] // HEAD_DIM == lane[None, :] // HEAD_DIM).astype(BF16)
    wr = jnp.zeros((d, LANES), F32).at[:, :N_EXPERTS].set(w_router)
    wrh = wr.astype(BF16)
    wrl = (wr - wrh.astype(F32)).astype(BF16)
    return dict(
        gmix=g_norm_mix.reshape(1, d), gffn=g_norm_ffn.reshape(1, d),
        wq=w_in[:, 0:aw].astype(BF16), wk=w_in[:, aw:2 * aw].astype(BF16),
        wv=w_in[:, 2 * aw:3 * aw].astype(BF16), wf=wf.astype(BF16), bf=bf,
        wu=w_in[:, o:o + sw].astype(BF16), wvg=w_in[:, o + sw:o + 2 * sw].astype(BF16),
        wga=w_in[:, o + 2 * sw:o + 2 * sw + d].astype(BF16),
        wgb=w_in[:, o + 2 * sw + d:o + 2 * sw + 2 * d].astype(BF16),
        gq=jnp.tile(g_q, N_HEADS).reshape(1, aw), gk=jnp.tile(g_k, N_HEADS).reshape(1, aw),
        gvn=g_vnorm.reshape(1, sw), bvn=b_vnorm.reshape(1, sw), bd=bd,
        wbb=w_branch_b.astype(BF16), wba=w_branch_a.astype(BF16), wo=w_out.astype(BF16),
        wrh=wrh, wrl=wrl, wsg=w_sh_gate.astype(BF16), wsu=w_sh_up.astype(BF16),
        wsd=w_sh_down.astype(BF16))


def _spatial_weights(w_spatial, b_spatial, rows_are_sequences, tl):
    if rows_are_sequences:
        wsp = w_spatial[:, 0, 0][:, None, None] * jnp.eye(CHUNK, dtype=F32)[None]
        b = jnp.broadcast_to(b_spatial[:, 0:1], (SGU_GROUPS, CHUNK))
    else:
        wsp = jnp.where(jnp.tril(jnp.ones((CHUNK, CHUNK), bool)), w_spatial, 0)
        b = b_spatial
    half = LANES // 2
    bsp = jnp.repeat(b.reshape(SGU_GROUPS // 2, 2, CHUNK), half, axis=1)
    bsp = bsp.transpose(0, 2, 1)
    r = jnp.arange(tl)
    ltri = (r[:, None] >= r[None, :]).astype(BF16)
    return dict(wsp=wsp.astype(BF16), bsp=bsp, ltri=ltri)


def _layer(x, mod, attend, wts, w_spatial, b_spatial, b_router, w_exp_gate, w_exp_up, w_exp_down,
           rows_are_sequences, tl, tm, tt):
    bx, l, d = x.shape
    wts = dict(wts, **_spatial_weights(w_spatial, b_spatial, rows_are_sequences, tl))
    q, kf, vf, kb, vb, lf, fc, mb, sga, vn = _mix_in(x, mod, wts, tl)
    oa = attend(q, kf, vf, kb, vb, lf, fc)
    h2, logits, base = _mix_out(x, oa, sga, mb, mod, wts, tl)
    t = bx * l
    w_t, r_t, runs = _route(logits.reshape(t, LANES).T, b_router, tt)
    (tile_major, expert_major, run_len), plan, max_rows = _moe_plan(runs[:, :, 0], tt, tm)
    xs_tl = _local_sort(r_t, h2.reshape(t, d // 2), tt)
    xs = _regroup(tile_major, expert_major, run_len, xs_tl, None, max_rows)
    eo = _moe(plan, xs, w_exp_gate, w_exp_up, w_exp_down, tm)
    eo_tl = _regroup(expert_major, tile_major, run_len, eo, xs_tl, xs_tl.shape[0])
    y = _local_combine(r_t.T.reshape(bx, l, TOP_K), w_t.T.reshape(bx, l, TOP_K), eo_tl, base, mod, tt)
    return y, kf, vf, lf, vn


def kernel(x_prompt, x_sample, c_prompt, c_sample, cache_k, cache_v, cache_logf, page_table, w_ada, b_ada, g_norm_mix, g_norm_ffn, w_in, b_forget, g_q, g_k, g_vnorm, b_vnorm, w_spatial, b_spatial, w_branch_a, w_branch_b, w_out, w_router, b_router, w_exp_gate, w_exp_up, w_exp_down, w_sh_gate, w_sh_up, w_sh_down):
    assert w_ada.shape[0] == 1, "one layer"
    b, s, d = x_prompt.shape
    nb = x_sample.shape[0]

    c_all = jnp.concatenate([c_prompt, c_sample], axis=0)
    pad = (-c_all.shape[0]) % 8
    c_all = jnp.pad(c_all, ((0, pad), (0, 0)))
    mod_all = _ada(c_all, w_ada[0], b_ada[0])
    mod_p = mod_all[:b].reshape(b, 1, 6 * d)
    mod_s = mod_all[b:b + nb].reshape(1, nb, 6 * d)

    wts = _prep_weights(w_in[0], b_forget[0], g_q[0], g_k[0], g_vnorm[0], b_vnorm[0], g_norm_mix[0],
                        g_norm_ffn[0], w_branch_a[0], w_branch_b[0], w_out[0], w_router[0],
                        w_sh_gate[0], w_sh_up[0], w_sh_down[0])
    experts = (w_exp_gate[0], w_exp_up[0], w_exp_down[0])

    tq = min(512, s)

    def attend_prompt(q, kf, vf, kb, vb, lf, fc):
        return _attn_prompt(q, kb, vb, fc.transpose(0, 2, 1), tq)

    def attend_sample(q, kf, vf, kb, vb, lf, fc):
        def lane_rep(a):
            a = a.astype(F32).reshape(nb, N_HEADS, HEAD_DIM, 1)
            return jnp.broadcast_to(a, (nb, N_HEADS, HEAD_DIM, PAGE))

        o = _attn_decode(page_table, lane_rep(q), lane_rep(kf), lane_rep(vf),
                         lf.reshape(nb, N_HEADS, 1),
                         cache_logf.transpose(0, 1, 3, 2),
                         cache_k.transpose(0, 1, 3, 4, 2),
                         cache_v.transpose(0, 1, 3, 4, 2))
        return o[..., 0].reshape(1, nb, ATTN_WIDTH).astype(BF16)

    y_s, k_s, v_s, lf_s, vn_s = _layer(x_sample.reshape(1, nb, d), mod_s, attend_sample, wts,
                                       w_spatial[0], b_spatial[0], b_router[0], *experts,
                                       rows_are_sequences=True, tl=nb, tm=32, tt=nb)
    y_p, k_p, v_p, lf_p, _ = _layer(x_prompt, mod_p, attend_prompt, wts, w_spatial[0], b_spatial[0],
                                    b_router[0], *experts, rows_are_sequences=False,
                                    tl=min(512, s), tm=min(512, s), tt=min(256, s))
    hd5 = (1, b, s, N_HEADS, HEAD_DIM)
    sd5 = (1, nb, 1, N_HEADS, HEAD_DIM)
    return (y_p, y_s.reshape(nb, 1, d),
            k_p.reshape(hd5), v_p.reshape(hd5), lf_p.reshape(1, b, s, N_HEADS),
            k_s.reshape(sd5), v_s.reshape(sd5), lf_s.reshape(1, nb, 1, N_HEADS),
            vn_s.reshape(1, nb, 1, SGU_WIDTH))
```

```python
import functools

import jax
import jax.numpy as jnp
from jax import lax
from jax.experimental import pallas as pl
from jax.experimental.pallas import tpu as pltpu

F32 = jnp.float32
BF16 = jnp.bfloat16
I32 = jnp.int32
U32 = jnp.uint32

D_MODEL = 1024
N_HEADS = 8
HEAD_DIM = 64
ATTN_WIDTH = N_HEADS * HEAD_DIM
SGU_GROUPS = 8
SGU_WIDTH = 512
CHUNK = 128
N_EXPERTS = 64
TOP_K = 8
N_EXPERT_GROUPS = 8
TOP_K_GROUPS = 4
EXPERTS_PER_GROUP = 8
D_EXPERT = 256
D_SHARED = 256
ROUTED_SCALE = 2.5
NORM_EPS = 1e-6
ATTN_SCALE = HEAD_DIM ** -0.5
LOG2E = 1.4426950408889634
PAGE = 128
LANES = 128
VMEM_LIMIT = 56 * 1024 * 1024
PAGES_PER_STEP = 16
SEG_ALIGN = 8
SORT_CHUNK = 512

_dot = functools.partial(jnp.dot, preferred_element_type=F32)


def _dot_nt(a, b):
    return lax.dot_general(a, b, (((1,), (1,)), ((), ())), preferred_element_type=F32)


def _sigmoid(x):
    return 1.0 / (1.0 + jnp.exp(-x))


def _gelu(x):
    return 0.5 * x * (1.0 + jnp.tanh(0.7978845608028654 * (x + 0.044715 * (x * x * x))))


def _split3(x):
    hi = x.astype(BF16)
    r1 = x - hi.astype(F32)
    mid = r1.astype(BF16)
    lo = (r1 - mid.astype(F32)).astype(BF16)
    return hi, mid, lo


def _pack_halves(x):
    w = x.shape[1] // 2
    lo = lax.bitcast_convert_type(x[:, :w].astype(BF16).astype(F32), U32)
    hi = lax.bitcast_convert_type(x[:, w:].astype(BF16).astype(F32), U32)
    return (hi & jnp.uint32(0xFFFF0000)) | (lo >> 16)


def _unpack_halves(p):
    return (lax.bitcast_convert_type(p << 16, F32),
            lax.bitcast_convert_type(p & jnp.uint32(0xFFFF0000), F32))


def _dot3_left(m_bf16, x):
    hi, mid, lo = _split3(x)
    return _dot(m_bf16, hi) + _dot(m_bf16, mid) + _dot(m_bf16, lo)


def _dot3_right(x, m_bf16):
    hi, mid, lo = _split3(x)
    return _dot(hi, m_bf16) + _dot(mid, m_bf16) + _dot(lo, m_bf16)


def _params(*sem, row_dma=False):
    return pltpu.CompilerParams(dimension_semantics=sem, vmem_limit_bytes=VMEM_LIMIT,
                                disable_bounds_checks=row_dma)


def _const_spec(shape):
    zeros = (0,) * len(shape)
    return pl.BlockSpec(shape, lambda *_: zeros)


def _ada_kernel(c_ref, w_ref, b_ref, o_ref):
    c = c_ref[...]
    a = c * _sigmoid(c)
    o_ref[...] = jnp.dot(a, w_ref[...], preferred_element_type=F32,
                         precision=lax.Precision.HIGHEST) + b_ref[...]


def _ada(c, w_ada, b_ada):
    m, d = c.shape
    n = w_ada.shape[1]
    tn = 1024
    return pl.pallas_call(
        _ada_kernel,
        grid=(n // tn,),
        in_specs=[pl.BlockSpec((m, d), lambda j: (0, 0)),
                  pl.BlockSpec((d, tn), lambda j: (0, j)),
                  pl.BlockSpec((1, tn), lambda j: (0, j))],
        out_specs=pl.BlockSpec((m, tn), lambda j: (0, j)),
        out_shape=jax.ShapeDtypeStruct((m, n), F32),
        compiler_params=_params("arbitrary"),
        name="ada",
    )(c, w_ada, b_ada.reshape(1, n))


def _mix_in_kernel(x_ref, mod_ref, gmix_ref, wq_ref, wk_ref, wv_ref, wf_ref, wu_ref, wvg_ref,
                   wga_ref, wgb_ref, bf_ref, gq_ref, gk_ref, gvn_ref, bvn_ref, bd_ref,
                   wsp_ref, bsp_ref, wbb_ref, ltri_ref,
                   q_ref, kf_ref, vf_ref, kb_ref, vb_ref, lf_ref, fc_ref, mb_ref, sga_ref, vn_ref,
                   carry_sc):
    tl = x_ref.shape[1]
    x = x_ref[0]
    shift1 = mod_ref[0, :, 0:D_MODEL]
    scale1 = mod_ref[0, :, D_MODEL:2 * D_MODEL]
    ms = jnp.mean(x * x, axis=-1, keepdims=True)
    h = x * lax.rsqrt(ms + NORM_EPS) * gmix_ref[...] * (1.0 + scale1) + shift1
    hb = h.astype(BF16)
    bd = bd_ref[...]

    def head_norm(z, g):
        ss = _dot((z * z).astype(BF16), bd) * (1.0 / HEAD_DIM)
        return z * lax.rsqrt(ss + NORM_EPS) * g

    qn = head_norm(_dot(hb, wq_ref[...]), gq_ref[...]) * (ATTN_SCALE * LOG2E)
    q_ref[0] = qn.astype(BF16)
    kn = head_norm(_dot(hb, wk_ref[...]), gk_ref[...])
    kf_ref[0] = kn
    kb_ref[0] = kn.astype(BF16)
    v = _dot(hb, wv_ref[...])
    vf_ref[0] = v
    vb_ref[0] = v.astype(BF16)

    zf = _dot(hb, wf_ref[...]) + bf_ref[...]
    lf = jnp.minimum(zf, 0.0) - jnp.log(1.0 + jnp.exp(-jnp.abs(zf)))
    lf_ref[0] = lf[:, :N_HEADS]

    @pl.when(pl.program_id(1) == 0)
    def _():
        carry_sc[...] = jnp.zeros_like(carry_sc)

    fc = _dot3_left(ltri_ref[...], lf) + carry_sc[...]
    fc_ref[0] = fc[:, :N_HEADS]
    carry_sc[...] = fc[tl - 1:tl, :]

    gu = _gelu(_dot(hb, wu_ref[...]))
    gv = _gelu(_dot(hb, wvg_ref[...]))
    mu = jnp.mean(gv, axis=-1, keepdims=True)
    gc = gv - mu
    var = jnp.mean(gc * gc, axis=-1, keepdims=True)
    vn = gc * lax.rsqrt(var + NORM_EPS) * gvn_ref[...] + bvn_ref[...]
    vn_ref[0] = vn
    vnb = vn.astype(BF16)
    lane = lax.broadcasted_iota(I32, (CHUNK, LANES), 1)
    low = lane < (LANES // 2)
    zero = jnp.zeros((CHUNK, LANES), BF16)
    rows = []
    for c in range(tl // CHUNK):
        pieces = []
        for j in range(SGU_WIDTH // LANES):
            vp = vnb[c * CHUNK:(c + 1) * CHUNK, j * LANES:(j + 1) * LANES]
            mixed = (_dot(wsp_ref[2 * j], jnp.where(low, vp, zero))
                     + _dot(wsp_ref[2 * j + 1], jnp.where(low, zero, vp)) + bsp_ref[j])
            pieces.append(mixed)
        rows.append(jnp.concatenate(pieces, axis=1))
    mixed = rows[0] if len(rows) == 1 else jnp.concatenate(rows, axis=0)
    ob = (gu * mixed).astype(BF16)
    mb = _sigmoid(_dot(hb, wgb_ref[...])) * _dot(ob, wbb_ref[...])
    mb_ref[0] = mb.astype(BF16)
    sga_ref[0] = _sigmoid(_dot(hb, wga_ref[...])).astype(BF16)


def _mix_in(x, mod, wts, tl):
    bx, l, d = x.shape
    tlm = tl if mod.shape[1] > 1 else 1
    grid = (bx, l // tl)
    row = lambda b, i: (b, i, 0)
    mod_map = row if tlm > 1 else (lambda b, i: (b, 0, 0))
    names = ("gmix", "wq", "wk", "wv", "wf", "wu", "wvg", "wga", "wgb", "bf", "gq", "gk", "gvn",
             "bvn", "bd", "wsp", "bsp", "wbb", "ltri")
    consts = [wts[n] for n in names]
    out_widths = [(ATTN_WIDTH, BF16), (ATTN_WIDTH, F32), (ATTN_WIDTH, F32), (ATTN_WIDTH, BF16),
                  (ATTN_WIDTH, BF16), (N_HEADS, F32), (N_HEADS, F32), (D_MODEL, BF16),
                  (D_MODEL, BF16), (SGU_WIDTH, F32)]
    return pl.pallas_call(
        _mix_in_kernel,
        grid=grid,
        in_specs=[pl.BlockSpec((1, tl, d), row), pl.BlockSpec((1, tlm, 6 * d), mod_map)]
                 + [_const_spec(c.shape) for c in consts],
        out_specs=[pl.BlockSpec((1, tl, w), row) for w, _ in out_widths],
        out_shape=[jax.ShapeDtypeStruct((bx, l, w), dt) for w, dt in out_widths],
        scratch_shapes=[pltpu.VMEM((1, LANES), F32)],
        compiler_params=_params("arbitrary", "arbitrary"),
        name="mix_in",
    )(x, mod, *consts)


def _attn_kernel(qi_ref, ki_ref, q_ref, k_ref, v_ref, fq_ref, fk_ref, o_ref, m_sc, l_sc, acc_sc):
    tq = q_ref.shape[1]
    tk = k_ref.shape[1]
    pair = pl.program_id(1)
    qi = qi_ref[pair]
    ki = ki_ref[pair]

    @pl.when(ki == 0)
    def _():
        m_sc[...] = jnp.full_like(m_sc, -jnp.inf)
        l_sc[...] = jnp.zeros_like(l_sc)
        acc_sc[...] = jnp.zeros_like(acc_sc)

    lane = lax.broadcasted_iota(I32, (tq, LANES), 1)
    low = lane < HEAD_DIM

    def step(masked):
        ones = jnp.ones((tk, LANES), BF16)
        if masked:
            causal = (lax.broadcasted_iota(I32, (tq, tk), 0) >= lax.broadcasted_iota(I32, (tq, tk), 1))
        for j in range(ATTN_WIDTH // LANES):
            sl = slice(j * LANES, (j + 1) * LANES)
            qp = q_ref[0, :, sl]
            kp = k_ref[0, :, sl]
            vx = jnp.concatenate([v_ref[0, :, sl], ones], axis=1)
            zero = jnp.zeros_like(qp)
            alphas = []
            pvs = []
            for t in range(2):
                hd = 2 * j + t
                qh = jnp.where(low, qp, zero) if t == 0 else jnp.where(low, zero, qp)
                decay = (fq_ref[0, hd:hd + 1, 0:1] - fk_ref[0, hd:hd + 1, :]) * LOG2E
                s = _dot_nt(qh, kp) + decay
                if masked:
                    s = jnp.where(causal, s, -jnp.inf)
                m_prev = m_sc[hd]
                m_new = jnp.maximum(m_prev, jnp.max(s, axis=-1, keepdims=True))
                alpha = jnp.exp2(m_prev - m_new)
                p = jnp.concatenate([jnp.exp2(s[:, c * LANES:(c + 1) * LANES] - m_new)
                                     for c in range(tk // LANES)], axis=1)
                pv = _dot(p.astype(BF16), vx)
                l_sc[hd] = alpha * l_sc[hd] + pv[:, LANES:]
                m_sc[hd] = m_new
                alphas.append(alpha)
                pvs.append(pv[:, :LANES])
            acc_sc[j] = (acc_sc[j] * jnp.where(low, alphas[0], alphas[1])
                         + jnp.where(low, pvs[0], pvs[1]))

    @pl.when(ki < qi)
    def _():
        step(False)

    @pl.when(ki == qi)
    def _():
        step(True)
        for j in range(ATTN_WIDTH // LANES):
            inv = jnp.where(low, 1.0 / l_sc[2 * j], 1.0 / l_sc[2 * j + 1])
            o_ref[0, :, j * LANES:(j + 1) * LANES] = (acc_sc[j] * inv).astype(o_ref.dtype)


def _attn_prompt(q, k, v, fr, tq):
    b, s, w = q.shape
    nq = s // tq
    pairs = [(qi, ki) for qi in range(nq) for ki in range(qi + 1)]
    qi_arr = jnp.asarray([p[0] for p in pairs], I32)
    ki_arr = jnp.asarray([p[1] for p in pairs], I32)
    qmap = lambda bi, p, qa, ka: (bi, qa[p], 0)
    kmap = lambda bi, p, qa, ka: (bi, ka[p], 0)
    grid_spec = pltpu.PrefetchScalarGridSpec(
        num_scalar_prefetch=2,
        grid=(b, len(pairs)),
        in_specs=[pl.BlockSpec((1, tq, w), qmap),
                  pl.BlockSpec((1, tq, w), kmap),
                  pl.BlockSpec((1, tq, w), kmap),
                  pl.BlockSpec((1, N_HEADS, tq), lambda bi, p, qa, ka: (bi, 0, qa[p])),
                  pl.BlockSpec((1, N_HEADS, tq), lambda bi, p, qa, ka: (bi, 0, ka[p]))],
        out_specs=pl.BlockSpec((1, tq, w), qmap),
        scratch_shapes=[pltpu.VMEM((N_HEADS, tq, LANES), F32), pltpu.VMEM((N_HEADS, tq, LANES), F32),
                        pltpu.VMEM((w // LANES, tq, LANES), F32)],
    )
    return pl.pallas_call(
        _attn_kernel,
        grid_spec=grid_spec,
        out_shape=jax.ShapeDtypeStruct((b, s, w), BF16),
        compiler_params=_params("arbitrary", "arbitrary"),
        name="attn_prompt",
    )(qi_arr, ki_arr, q, k, v, fr, fr)


def _rows_to_tile(rows):
    sub = lax.broadcasted_iota(I32, (N_HEADS, PAGE), 0)
    tile = jnp.zeros((N_HEADS, PAGE), F32)
    for h, r in enumerate(rows):
        tile = jnp.where(sub == h, r, tile)
    return tile


def _attn_decode_kernel(pt_ref, q_ref, kn_ref, vn_ref, lfn_ref, u_ref, *refs):
    n = PAGES_PER_STEP
    lf_refs, k_refs, v_refs = refs[:n], refs[n:2 * n], refs[2 * n:3 * n]
    o_ref, m_sc, l_sc, acc_sc, carry_sc = refs[3 * n:]
    j = pl.program_id(1)
    nj = pl.num_programs(1)
    heads = range(N_HEADS)

    @pl.when(j == 0)
    def _():
        m_sc[...] = jnp.full_like(m_sc, -jnp.inf)
        l_sc[...] = jnp.zeros_like(l_sc)
        acc_sc[...] = jnp.zeros_like(acc_sc)
        carry_sc[...] = jnp.zeros_like(carry_sc)

    lfn = lfn_ref[0]
    lf_all = jnp.concatenate([lf_refs[i][0, 0] for i in range(n)], axis=0)
    later_all = _dot3_right(lf_all, u_ref[...])
    carry = carry_sc[...]
    decays = []
    for i in range(n):
        sl = slice(i * N_HEADS, (i + 1) * N_HEADS)
        decays.append((later_all[sl] + carry + lfn) * LOG2E)
        carry = carry + jnp.sum(lf_all[sl], axis=-1, keepdims=True)
    carry_sc[...] = carry

    rows = [[None] * N_HEADS for _ in range(n)]
    for h in heads:
        qh = q_ref[0, h]
        for i in range(n):
            rows[i][h] = jnp.sum(k_refs[i][0, 0, h] * qh, axis=0, keepdims=True)
    s = [_rows_to_tile(rows[i]) + decays[i] for i in range(n)]
    m = m_sc[...]
    m_new = m
    for i in range(n):
        m_new = jnp.maximum(m_new, jnp.max(s[i], axis=-1, keepdims=True))
    alpha = jnp.exp2(m - m_new)
    p = [jnp.exp2(s[i] - m_new) for i in range(n)]
    l = alpha * l_sc[...]
    for i in range(n):
        l = l + jnp.sum(p[i], axis=-1, keepdims=True)
    m_sc[...] = m_new
    l_sc[...] = l
    alpha_rep = jnp.broadcast_to(alpha, (N_HEADS, PAGE))
    for h in heads:
        a = acc_sc[h] * alpha_rep[h:h + 1, :]
        for i in range(n):
            a = a + p[i][h:h + 1, :] * v_refs[i][0, 0, h]
        acc_sc[h] = a

    @pl.when(j == nj - 1)
    def _():
        s_n = _rows_to_tile([jnp.sum(q_ref[0, h] * kn_ref[0, h], axis=0, keepdims=True) for h in heads])
        m_rep = jnp.broadcast_to(m_new, (N_HEADS, PAGE))
        m_fin = jnp.maximum(m_rep, s_n)
        a_fin = jnp.exp2(m_rep - m_fin)
        p_n = jnp.exp2(s_n - m_fin)
        inv = 1.0 / (a_fin * jnp.broadcast_to(l, (N_HEADS, PAGE)) + p_n)
        for h in heads:
            total = jnp.broadcast_to(jnp.sum(acc_sc[h], axis=-1, keepdims=True), (HEAD_DIM, PAGE))
            o_ref[0, h] = (a_fin[h:h + 1, :] * total + p_n[h:h + 1, :] * vn_ref[0, h]) * inv[h:h + 1, :]


def _attn_decode(page_table, q_rep, k_new_rep, v_new_rep, lf_new, cache_lf_t, cache_k_t, cache_v_t):
    nb, n_pages = page_table.shape
    n = PAGES_PER_STEP
    tok3 = lambda b, j, pt: (b, 0, 0)
    tok4 = lambda b, j, pt: (b, 0, 0, 0)

    def page(i, rank):
        def index_map(b, j, pt):
            return (0, pt[b * n_pages + n_pages - 1 - (j * n + i)]) + (0,) * (rank - 2)
        return index_map

    lane = jnp.arange(PAGE)
    later = (lane[:, None] > lane[None, :]).astype(BF16)
    rep = pl.BlockSpec((1, N_HEADS, HEAD_DIM, PAGE), tok4)
    grid_spec = pltpu.PrefetchScalarGridSpec(
        num_scalar_prefetch=1,
        grid=(nb, n_pages // n),
        in_specs=[rep, rep, rep,
                  pl.BlockSpec((1, N_HEADS, 1), tok3),
                  pl.BlockSpec((PAGE, PAGE), lambda b, j, pt: (0, 0))]
                 + [pl.BlockSpec((1, 1, N_HEADS, PAGE), page(i, 4)) for i in range(n)]
                 + [pl.BlockSpec((1, 1, N_HEADS, HEAD_DIM, PAGE), page(i, 5)) for i in range(n)]
                 + [pl.BlockSpec((1, 1, N_HEADS, HEAD_DIM, PAGE), page(i, 5)) for i in range(n)],
        out_specs=rep,
        scratch_shapes=[pltpu.VMEM((N_HEADS, 1), F32), pltpu.VMEM((N_HEADS, 1), F32),
                        pltpu.VMEM((N_HEADS, HEAD_DIM, PAGE), F32), pltpu.VMEM((N_HEADS, 1), F32)],
    )
    return pl.pallas_call(
        _attn_decode_kernel,
        grid_spec=grid_spec,
        out_shape=jax.ShapeDtypeStruct((nb, N_HEADS, HEAD_DIM, PAGE), F32),
        compiler_params=_params("arbitrary", "arbitrary"),
        name="attn_decode",
    )(page_table.reshape(-1), q_rep, k_new_rep, v_new_rep, lf_new, later,
      *([cache_lf_t] * n), *([cache_k_t] * n), *([cache_v_t] * n))


def _mix_out_kernel(x_ref, oa_ref, sga_ref, mb_ref, mod_ref, wba_ref, wo_ref, gffn_ref,
                    wrh_ref, wrl_ref, wsg_ref, wsu_ref, wsd_ref,
                    h2_ref, lg_ref, base_ref):
    d = D_MODEL
    x = x_ref[0]
    gate1 = mod_ref[0, :, 2 * d:3 * d]
    shift2 = mod_ref[0, :, 3 * d:4 * d]
    scale2 = mod_ref[0, :, 4 * d:5 * d]
    gate2 = mod_ref[0, :, 5 * d:6 * d]
    merged = sga_ref[0].astype(F32) * _dot(oa_ref[0], wba_ref[...]) + mb_ref[0].astype(F32)
    x1 = x + gate1 * _dot(merged.astype(BF16), wo_ref[...])
    ms = jnp.mean(x1 * x1, axis=-1, keepdims=True)
    h2 = x1 * lax.rsqrt(ms + NORM_EPS) * gffn_ref[...] * (1.0 + scale2) + shift2
    h2_ref[0] = _pack_halves(h2)
    hb = h2.astype(BF16)
    hl = (h2 - hb.astype(F32)).astype(BF16)
    lg_ref[0] = _dot(hb, wrh_ref[...]) + (_dot(hb, wrl_ref[...]) + _dot(hl, wrh_ref[...]))
    g = _dot(hb, wsg_ref[...])
    u = _dot(hb, wsu_ref[...])
    a = (g * _sigmoid(g) * u).astype(BF16)
    base_ref[0] = x1 + gate2 * _dot(a, wsd_ref[...])


def _mix_out(x, oa, sga, mb, mod, wts, tl):
    bx, l, d = x.shape
    tlm = tl if mod.shape[1] > 1 else 1
    row = lambda b, i: (b, i, 0)
    mod_map = row if tlm > 1 else (lambda b, i: (b, 0, 0))
    names = ("wba", "wo", "gffn", "wrh", "wrl", "wsg", "wsu", "wsd")
    consts = [wts[n] for n in names]
    return pl.pallas_call(
        _mix_out_kernel,
        grid=(bx, l // tl),
        in_specs=[pl.BlockSpec((1, tl, d), row), pl.BlockSpec((1, tl, ATTN_WIDTH), row),
                  pl.BlockSpec((1, tl, d), row), pl.BlockSpec((1, tl, d), row),
                  pl.BlockSpec((1, tlm, 6 * d), mod_map)] + [_const_spec(c.shape) for c in consts],
        out_specs=[pl.BlockSpec((1, tl, d // 2), row), pl.BlockSpec((1, tl, LANES), row),
                   pl.BlockSpec((1, tl, d), row)],
        out_shape=[jax.ShapeDtypeStruct((bx, l, d // 2), U32), jax.ShapeDtypeStruct((bx, l, LANES), F32),
                   jax.ShapeDtypeStruct((bx, l, d), F32)],
        compiler_params=_params("arbitrary", "arbitrary"),
        name="mix_out",
    )(x, oa, sga, mb, mod, *consts)


def _route_kernel(lg_ref, b_ref, before_ref, lower_ref, w_ref, r_ref, cnt_ref):
    tt = lg_ref.shape[1]
    epg = EXPERTS_PER_GROUP
    ninf = -jnp.inf
    iota = lax.broadcasted_iota(I32, (epg, tt), 0)
    sc = []
    biased = []
    gscore = []
    for g in range(N_EXPERT_GROUPS):
        s = _sigmoid(lg_ref[g * epg:(g + 1) * epg, :])
        bz = s + b_ref[g * epg:(g + 1) * epg, :]
        m1 = jnp.max(bz, axis=0, keepdims=True)
        first = jnp.min(jnp.where(bz == m1, iota, epg), axis=0, keepdims=True)
        m2 = jnp.max(jnp.where(iota == first, ninf, bz), axis=0, keepdims=True)
        sc.append(s)
        biased.append(bz)
        gscore.append(m1 + m2)
    cand = []
    for g in range(N_EXPERT_GROUPS):
        rank = jnp.zeros((1, tt), I32)
        for o in range(N_EXPERT_GROUPS):
            if o == g:
                continue
            beats = (gscore[o] >= gscore[g]) if o < g else (gscore[o] > gscore[g])
            rank = rank + beats.astype(I32)
        cand.append(jnp.where(rank < TOP_K_GROUPS, biased[g], ninf))
    ws = []
    picks = []
    for k in range(TOP_K):
        mx = cand[0]
        for g in range(1, N_EXPERT_GROUPS):
            mx = jnp.maximum(mx, cand[g])
        mx = jnp.max(mx, axis=0, keepdims=True)
        fi = jnp.where(cand[0] == mx, iota, N_EXPERTS)
        for g in range(1, N_EXPERT_GROUPS):
            fi = jnp.minimum(fi, jnp.where(cand[g] == mx, iota + g * epg, N_EXPERTS))
        fi = jnp.min(fi, axis=0, keepdims=True)
        wk = jnp.zeros((epg, tt), F32)
        for g in range(N_EXPERT_GROUPS):
            hit = (iota + g * epg) == fi
            wk = wk + jnp.where(hit, sc[g], 0.0)
            cand[g] = jnp.where(hit, ninf, cand[g])
        picks.append(fi)
        ws.append(jnp.sum(wk, axis=0, keepdims=True))
    tot = ws[0]
    for k in range(1, TOP_K):
        tot = tot + ws[k]
    for k in range(TOP_K):
        w_ref[k:k + 1, :] = ws[k] / tot * ROUTED_SCALE

    chosen = []
    for g in range(N_EXPERT_GROUPS):
        sel = jnp.zeros((epg, tt), F32)
        for k in range(TOP_K):
            sel = sel + jnp.where((iota + g * epg) == picks[k], 1.0, 0.0)
        chosen.append(sel)
    chosen = jnp.concatenate(chosen, axis=0)
    earlier = _dot(chosen.astype(BF16), before_ref[...])
    run = jnp.floor((jnp.sum(chosen, axis=1, keepdims=True) + (SEG_ALIGN - 1)) * (1.0 / SEG_ALIGN)) * SEG_ALIGN
    run_rep = jnp.broadcast_to(run, (N_EXPERTS, LANES))
    cnt_ref[0] = run_rep
    start = _dot(lower_ref[...], run_rep.astype(BF16))[:, 0:1]
    local = earlier + start
    for k in range(TOP_K):
        ck = jnp.zeros((epg, tt), F32)
        for g in range(N_EXPERT_GROUPS):
            ck = ck + jnp.where((iota + g * epg) == picks[k], local[g * epg:(g + 1) * epg, :], 0.0)
        r_ref[k:k + 1, :] = jnp.sum(ck, axis=0, keepdims=True).astype(I32)


def _route(logits_t, b_router, tt):
    t = logits_t.shape[1]
    r = jnp.arange(tt)
    before = (r[:, None] < r[None, :]).astype(BF16)
    e = jnp.arange(N_EXPERTS)
    lower = (e[None, :] < e[:, None]).astype(BF16)
    col = lambda i: (0, i)
    return pl.pallas_call(
        _route_kernel,
        grid=(t // tt,),
        in_specs=[pl.BlockSpec((N_EXPERTS, tt), col),
                  pl.BlockSpec((N_EXPERTS, 1), lambda i: (0, 0)),
                  pl.BlockSpec((tt, tt), lambda i: (0, 0)),
                  pl.BlockSpec((N_EXPERTS, N_EXPERTS), lambda i: (0, 0))],
        out_specs=[pl.BlockSpec((TOP_K, tt), col), pl.BlockSpec((TOP_K, tt), col),
                   pl.BlockSpec((1, N_EXPERTS, LANES), lambda i: (i, 0, 0))],
        out_shape=[jax.ShapeDtypeStruct((TOP_K, t), F32), jax.ShapeDtypeStruct((TOP_K, t), I32),
                   jax.ShapeDtypeStruct((t // tt, N_EXPERTS, LANES), F32)],
        compiler_params=_params("arbitrary"),
        name="route",
    )(logits_t, b_router.reshape(N_EXPERTS, 1), before, lower)


def _local_sort_kernel(to_ref, r_ref, h_ref, xs_hbm, buf, sem):
    tt = h_ref.shape[0]
    h = jnp.concatenate(_unpack_halves(h_ref[...]), axis=1).astype(BF16)
    for c in range(buf.shape[0] // SORT_CHUNK):
        rows = c * SORT_CHUNK + lax.broadcasted_iota(I32, (SORT_CHUNK, tt), 0)
        hit = rows == r_ref[0:1, :]
        for k in range(1, TOP_K):
            hit = hit | (rows == r_ref[k:k + 1, :])
        place = jnp.where(hit, 1.0, 0.0).astype(BF16)
        buf[c * SORT_CHUNK:(c + 1) * SORT_CHUNK, :] = _pack_halves(_dot(place, h))

    def copy(g):
        return pltpu.make_async_copy(buf.at[pl.ds(pl.multiple_of(g * SEG_ALIGN, SEG_ALIGN), SEG_ALIGN)],
                                     xs_hbm.at[pl.ds(pl.multiple_of(to_ref[0, 0, g], SEG_ALIGN), SEG_ALIGN)],
                                     sem.at[0])

    def issue(g, c):
        copy(g).start()
        return c

    def wait(g, c):
        copy(g).wait()
        return c

    used = to_ref[0, 0, buf.shape[0] // SEG_ALIGN]
    lax.fori_loop(0, used, issue, 0)
    lax.fori_loop(0, used, wait, 0)


def _local_rows(tt):
    pad = N_EXPERTS * SEG_ALIGN
    return -(-(tt * TOP_K + pad) // SORT_CHUNK) * SORT_CHUNK


def _local_sort(to_expert, r_t, h2, tt, out_rows):
    t, dp = h2.shape
    rows = _local_rows(tt)
    n = t // tt
    return pl.pallas_call(
        _local_sort_kernel,
        grid=(n,),
        in_specs=[pl.BlockSpec((1, 1, rows // SEG_ALIGN + 1), lambda i: (i, 0, 0), memory_space=pltpu.SMEM),
                  pl.BlockSpec((TOP_K, tt), lambda i: (0, i)), pl.BlockSpec((tt, dp), lambda i: (i, 0))],
        out_specs=pl.BlockSpec(memory_space=pl.ANY),
        out_shape=jax.ShapeDtypeStruct((out_rows, dp), U32),
        scratch_shapes=[pltpu.VMEM((rows, dp), U32), pltpu.SemaphoreType.DMA((1,))],
        compiler_params=_params("arbitrary", row_dma=True),
        name="moe_local_sort",
    )(to_expert.reshape(n, 1, rows // SEG_ALIGN + 1), r_t, h2)


VISIT_ACTIVE = 1
VISIT_FIRST = 2


def _moe_kernel(vt_ref, ve_ref, vf_ref, gs_ref, ge_ref, x_ref, wg_ref, wu_ref, wd_ref, o_ref):
    v = pl.program_id(0)
    tm = x_ref.shape[0]
    flags = vf_ref[v]

    @pl.when(flags >= VISIT_ACTIVE)
    def _():
        e = ve_ref[v]
        x = jnp.concatenate(_unpack_halves(x_ref[...]), axis=1).astype(BF16)
        g = _dot(x, wg_ref[0].astype(BF16))
        u = _dot(x, wu_ref[0].astype(BF16))
        a = (g * _sigmoid(g) * u).astype(BF16)
        res = _pack_halves(_dot(a, wd_ref[0].astype(BF16)))
        row = vt_ref[v] * tm + lax.broadcasted_iota(I32, (tm, 1), 0)
        mine = (row >= gs_ref[e]) & (row < ge_ref[e])

        @pl.when(flags >= VISIT_FIRST)
        def _():
            o_ref[...] = jnp.where(mine, res, jnp.zeros_like(res))

        @pl.when(flags < VISIT_FIRST)
        def _():
            o_ref[...] = jnp.where(mine, res, o_ref[...])


def _moe(plan, xs, w_gate, w_up, w_down, tm):
    visit_tile, visit_e, visit_flags, gs, ge = plan
    n_visits = visit_tile.shape[0]
    rows, dp = xs.shape
    d = 2 * dp
    tile = lambda v, vt, ve, vf, s, e: (vt[v], 0)
    expert = lambda v, vt, ve, vf, s, e: (ve[v], 0, 0)
    grid_spec = pltpu.PrefetchScalarGridSpec(
        num_scalar_prefetch=5,
        grid=(n_visits,),
        in_specs=[pl.BlockSpec((tm, dp), tile),
                  pl.BlockSpec((1, d, D_EXPERT), expert),
                  pl.BlockSpec((1, d, D_EXPERT), expert),
                  pl.BlockSpec((1, D_EXPERT, d), expert)],
        out_specs=pl.BlockSpec((tm, dp), tile),
    )
    return pl.pallas_call(
        _moe_kernel,
        grid_spec=grid_spec,
        out_shape=jax.ShapeDtypeStruct((rows, dp), U32),
        compiler_params=_params("arbitrary"),
        name="moe_experts",
    )(visit_tile, visit_e, visit_flags, gs, ge, xs, w_gate, w_up, w_down)


def _local_combine_kernel(fr_ref, frn_ref, eo_hbm, r_ref, w_ref, base_ref, mod_ref, y_ref, buf, sem):
    tt = base_ref.shape[1]
    half = D_MODEL // 2
    i = pl.program_id(0) * pl.num_programs(1) + pl.program_id(1)
    n = pl.num_programs(0) * pl.num_programs(1)
    slot = i % 2
    groups = buf.shape[1] // SEG_ALIGN

    def copy(tab_ref, s, g):
        return pltpu.make_async_copy(eo_hbm.at[pl.ds(pl.multiple_of(tab_ref[0, 0, g], SEG_ALIGN), SEG_ALIGN)],
                                     buf.at[s, pl.ds(pl.multiple_of(g * SEG_ALIGN, SEG_ALIGN), SEG_ALIGN)],
                                     sem.at[s])

    def issue(tab_ref, s):
        def body(g, c):
            copy(tab_ref, s, g).start()
            return c
        lax.fori_loop(0, tab_ref[0, 0, groups], body, 0)

    @pl.when(i == 0)
    def _():
        buf[...] = jnp.zeros_like(buf)
        issue(fr_ref, 0)

    @pl.when(i + 1 < n)
    def _():
        issue(frn_ref, 1 - slot)

    def wait(g, c):
        copy(fr_ref, slot, g).wait()
        return c
    lax.fori_loop(0, fr_ref[0, 0, groups], wait, 0)

    e_ref = buf.at[slot]
    lo = jnp.zeros((tt, half), F32)
    hi = jnp.zeros((tt, half), F32)
    for c in range(e_ref.shape[0] // SORT_CHUNK):
        cols = c * SORT_CHUNK + lax.broadcasted_iota(I32, (tt, SORT_CHUNK), 1)
        wm = jnp.zeros((tt, SORT_CHUNK), F32)
        for k in range(TOP_K):
            wm = jnp.where(cols == r_ref[0, :, k:k + 1], w_ref[0, :, k:k + 1], wm)
        wh = wm.astype(BF16)
        wl = (wm - wh.astype(F32)).astype(BF16)
        e_lo, e_hi = _unpack_halves(e_ref[c * SORT_CHUNK:(c + 1) * SORT_CHUNK, :])
        e_lo, e_hi = e_lo.astype(BF16), e_hi.astype(BF16)
        lo = lo + (_dot(wh, e_lo) + _dot(wl, e_lo))
        hi = hi + (_dot(wh, e_hi) + _dot(wl, e_hi))
    gate2 = mod_ref[0, :, 5 * D_MODEL:6 * D_MODEL]
    y_ref[0, :, :half] = base_ref[0, :, :half] + gate2[:, :half] * lo
    y_ref[0, :, half:] = base_ref[0, :, half:] + gate2[:, half:] * hi


def _local_combine(from_expert, eo, r_tok, w_tok, base, mod, tt):
    bx, l, d = base.shape
    nl = l // tt
    n = bx * nl
    rows = _local_rows(tt)
    groups = rows // SEG_ALIGN
    tab = from_expert.reshape(n, 1, groups + 1)
    tlm = tt if mod.shape[1] > 1 else 1
    row = lambda b, i: (b, i, 0)
    mod_map = row if tlm > 1 else (lambda b, i: (b, 0, 0))
    return pl.pallas_call(
        _local_combine_kernel,
        grid=(bx, nl),
        in_specs=[pl.BlockSpec((1, 1, groups + 1), lambda b, i: (b * nl + i, 0, 0), memory_space=pltpu.SMEM),
                  pl.BlockSpec((1, 1, groups + 1), lambda b, i: (jnp.minimum(b * nl + i + 1, n - 1), 0, 0),
                               memory_space=pltpu.SMEM),
                  pl.BlockSpec(memory_space=pl.ANY),
                  pl.BlockSpec((1, tt, TOP_K), row), pl.BlockSpec((1, tt, TOP_K), row),
                  pl.BlockSpec((1, tt, d), row),
                  pl.BlockSpec((1, tlm, 6 * d), mod_map)],
        out_specs=pl.BlockSpec((1, tt, d), row),
        out_shape=jax.ShapeDtypeStruct((bx, l, d), F32),
        scratch_shapes=[pltpu.VMEM((2, rows, d // 2), U32), pltpu.SemaphoreType.DMA((2,))],
        compiler_params=_params("arbitrary", "arbitrary", row_dma=True),
        name="moe_local_combine",
    )(tab, tab, eo, r_tok, w_tok, base, mod)


def _moe_plan(runs, tt, tm):
    n_tok_tiles = runs.shape[0]
    runs = runs.astype(I32)
    local_start = jnp.cumsum(runs, axis=1) - runs
    totals = jnp.sum(runs, axis=0)
    ge = jnp.cumsum(totals)
    gs = ge - totals
    expert_major = gs[None, :] + jnp.cumsum(runs, axis=0) - runs
    local_row = jnp.arange(_local_rows(tt) // SEG_ALIGN, dtype=I32) * SEG_ALIGN
    run_of = jnp.sum((local_start + runs)[:, None, :] <= local_row[None, :, None], axis=2)
    in_run = run_of[:, :, None] == jnp.arange(N_EXPERTS, dtype=I32)[None, None, :]
    shift = jnp.sum(jnp.where(in_run, (expert_major - local_start)[:, None, :], 0), axis=2)
    table = jnp.where(run_of < N_EXPERTS, local_row[None, :] + shift, -1).astype(I32)
    table = jnp.concatenate([table, jnp.sum(runs, axis=1, keepdims=True) // SEG_ALIGN], axis=1)
    max_rows = -(-(n_tok_tiles * (tt * TOP_K + N_EXPERTS * (SEG_ALIGN - 1))) // tm) * tm
    n_tiles = max_rows // tm
    n_visits = n_tiles + N_EXPERTS
    experts = jnp.arange(N_EXPERTS, dtype=I32)
    first_tile = gs // tm
    n_vis = jnp.where(totals > 0, (ge - 1) // tm - first_tile + 1, 0)
    vend = jnp.cumsum(n_vis)
    vstart = vend - n_vis
    v = jnp.arange(n_visits, dtype=I32)
    active = v < vend[-1]
    e_of_v = jnp.minimum(jnp.sum(vend[None, :] <= v[:, None], axis=1), N_EXPERTS - 1).astype(I32)
    onehot = e_of_v[:, None] == experts[None, :]
    pick = lambda table: jnp.sum(jnp.where(onehot, table[None, :], 0), axis=1)
    last_tile = jnp.maximum(ge[-1] - 1, 0) // tm
    tile_of_v = jnp.where(active, pick(first_tile) + v - pick(vstart), last_tile).astype(I32)
    prev_tile = jnp.concatenate([jnp.full((1,), -1, I32), tile_of_v[:-1]])
    flags = jnp.where(active, VISIT_ACTIVE + VISIT_FIRST * (tile_of_v != prev_tile), 0).astype(I32)
    return table, (tile_of_v, e_of_v, flags, gs, ge), max_rows


def _prep_weights(w_in, b_forget, g_q, g_k, g_vnorm, b_vnorm, g_norm_mix, g_norm_ffn,
                  w_branch_a, w_branch_b, w_out, w_router, w_sh_gate, w_sh_up, w_sh_down):
    aw, sw, d = ATTN_WIDTH, SGU_WIDTH, D_MODEL
    o = 3 * aw + N_HEADS
    wf = jnp.zeros((d, LANES), F32).at[:, :N_HEADS].set(w_in[:, 3 * aw:o])
    bf = jnp.zeros((1, LANES), F32).at[0, :N_HEADS].set(b_forget)
    lane = jnp.arange(aw)
    bd = (lane[:, None] // HEAD_DIM == lane[None, :] // HEAD_DIM).astype(BF16)
    wr = jnp.zeros((d, LANES), F32).at[:, :N_EXPERTS].set(w_router)
    wrh = wr.astype(BF16)
    wrl = (wr - wrh.astype(F32)).astype(BF16)
    return dict(
        gmix=g_norm_mix.reshape(1, d), gffn=g_norm_ffn.reshape(1, d),
        wq=w_in[:, 0:aw].astype(BF16), wk=w_in[:, aw:2 * aw].astype(BF16),
        wv=w_in[:, 2 * aw:3 * aw].astype(BF16), wf=wf.astype(BF16), bf=bf,
        wu=w_in[:, o:o + sw].astype(BF16), wvg=w_in[:, o + sw:o + 2 * sw].astype(BF16),
        wga=w_in[:, o + 2 * sw:o + 2 * sw + d].astype(BF16),
        wgb=w_in[:, o + 2 * sw + d:o + 2 * sw + 2 * d].astype(BF16),
        gq=jnp.tile(g_q, N_HEADS).reshape(1, aw), gk=jnp.tile(g_k, N_HEADS).reshape(1, aw),
        gvn=g_vnorm.reshape(1, sw), bvn=b_vnorm.reshape(1, sw), bd=bd,
        wbb=w_branch_b.astype(BF16), wba=w_branch_a.astype(BF16), wo=w_out.astype(BF16),
        wrh=wrh, wrl=wrl, wsg=w_sh_gate.astype(BF16), wsu=w_sh_up.astype(BF16),
        wsd=w_sh_down.astype(BF16))


def _spatial_weights(w_spatial, b_spatial, rows_are_sequences, tl):
    if rows_are_sequences:
        wsp = w_spatial[:, 0, 0][:, None, None] * jnp.eye(CHUNK, dtype=F32)[None]
        b = jnp.broadcast_to(b_spatial[:, 0:1], (SGU_GROUPS, CHUNK))
    else:
        wsp = jnp.where(jnp.tril(jnp.ones((CHUNK, CHUNK), bool)), w_spatial, 0)
        b = b_spatial
    half = LANES // 2
    bsp = jnp.repeat(b.reshape(SGU_GROUPS // 2, 2, CHUNK), half, axis=1)
    bsp = bsp.transpose(0, 2, 1)
    r = jnp.arange(tl)
    ltri = (r[:, None] >= r[None, :]).astype(BF16)
    return dict(wsp=wsp.astype(BF16), bsp=bsp, ltri=ltri)


def _layer(x, mod, attend, wts, w_spatial, b_spatial, b_router, w_exp_gate, w_exp_up, w_exp_down,
           rows_are_sequences, tl, tm, tt):
    bx, l, d = x.shape
    wts = dict(wts, **_spatial_weights(w_spatial, b_spatial, rows_are_sequences, tl))
    q, kf, vf, kb, vb, lf, fc, mb, sga, vn = _mix_in(x, mod, wts, tl)
    oa = attend(q, kf, vf, kb, vb, lf, fc)
    h2, logits, base = _mix_out(x, oa, sga, mb, mod, wts, tl)
    t = bx * l
    w_t, r_t, runs = _route(logits.reshape(t, LANES).T, b_router, tt)
    table, plan, max_rows = _moe_plan(runs[:, :, 0], tt, tm)
    xs = _local_sort(table, r_t, h2.reshape(t, d // 2), tt, max_rows)
    eo = _moe(plan, xs, w_exp_gate, w_exp_up, w_exp_down, tm)
    y = _local_combine(table, eo, r_t.T.reshape(bx, l, TOP_K), w_t.T.reshape(bx, l, TOP_K), base, mod, tt)
    return y, kf, vf, lf, vn


def kernel(x_prompt, x_sample, c_prompt, c_sample, cache_k, cache_v, cache_logf, page_table, w_ada, b_ada, g_norm_mix, g_norm_ffn, w_in, b_forget, g_q, g_k, g_vnorm, b_vnorm, w_spatial, b_spatial, w_branch_a, w_branch_b, w_out, w_router, b_router, w_exp_gate, w_exp_up, w_exp_down, w_sh_gate, w_sh_up, w_sh_down):
    assert w_ada.shape[0] == 1, "one layer"
    b, s, d = x_prompt.shape
    nb = x_sample.shape[0]

    c_all = jnp.concatenate([c_prompt, c_sample], axis=0)
    pad = (-c_all.shape[0]) % 8
    c_all = jnp.pad(c_all, ((0, pad), (0, 0)))
    mod_all = _ada(c_all, w_ada[0], b_ada[0])
    mod_p = mod_all[:b].reshape(b, 1, 6 * d)
    mod_s = mod_all[b:b + nb].reshape(1, nb, 6 * d)

    wts = _prep_weights(w_in[0], b_forget[0], g_q[0], g_k[0], g_vnorm[0], b_vnorm[0], g_norm_mix[0],
                        g_norm_ffn[0], w_branch_a[0], w_branch_b[0], w_out[0], w_router[0],
                        w_sh_gate[0], w_sh_up[0], w_sh_down[0])
    experts = (w_exp_gate[0], w_exp_up[0], w_exp_down[0])

    tq = min(512, s)

    def attend_prompt(q, kf, vf, kb, vb, lf, fc):
        return _attn_prompt(q, kb, vb, fc.transpose(0, 2, 1), tq)

    def attend_sample(q, kf, vf, kb, vb, lf, fc):
        def lane_rep(a):
            a = a.astype(F32).reshape(nb, N_HEADS, HEAD_DIM, 1)
            return jnp.broadcast_to(a, (nb, N_HEADS, HEAD_DIM, PAGE))

        o = _attn_decode(page_table, lane_rep(q), lane_rep(kf), lane_rep(vf),
                         lf.reshape(nb, N_HEADS, 1),
                         cache_logf.transpose(0, 1, 3, 2),
                         cache_k.transpose(0, 1, 3, 4, 2),
                         cache_v.transpose(0, 1, 3, 4, 2))
        return o[..., 0].reshape(1, nb, ATTN_WIDTH).astype(BF16)

    y_s, k_s, v_s, lf_s, vn_s = _layer(x_sample.reshape(1, nb, d), mod_s, attend_sample, wts,
                                       w_spatial[0], b_spatial[0], b_router[0], *experts,
                                       rows_are_sequences=True, tl=nb, tm=32, tt=nb)
    y_p, k_p, v_p, lf_p, _ = _layer(x_prompt, mod_p, attend_prompt, wts, w_spatial[0], b_spatial[0],
                                    b_router[0], *experts, rows_are_sequences=False,
                                    tl=min(512, s), tm=min(512, s), tt=min(256, s))
    hd5 = (1, b, s, N_HEADS, HEAD_DIM)
    sd5 = (1, nb, 1, N_HEADS, HEAD_DIM)
    return (y_p, y_s.reshape(nb, 1, d),
            k_p.reshape(hd5), v_p.reshape(hd5), lf_p.reshape(1, b, s, N_HEADS),
            k_s.reshape(sd5), v_s.reshape(sd5), lf_s.reshape(1, nb, 1, N_HEADS),
            vn_s.reshape(1, nb, 1, SGU_WIDTH))
```

```python
import functools

import jax
import jax.numpy as jnp
from jax import lax
from jax.experimental import pallas as pl
from jax.experimental.pallas import tpu as pltpu

F32 = jnp.float32
BF16 = jnp.bfloat16
I32 = jnp.int32
U32 = jnp.uint32

D_MODEL = 1024
N_HEADS = 8
HEAD_DIM = 64
ATTN_WIDTH = N_HEADS * HEAD_DIM
SGU_GROUPS = 8
SGU_WIDTH = 512
CHUNK = 128
N_EXPERTS = 64
TOP_K = 8
N_EXPERT_GROUPS = 8
TOP_K_GROUPS = 4
EXPERTS_PER_GROUP = 8
D_EXPERT = 256
D_SHARED = 256
ROUTED_SCALE = 2.5
NORM_EPS = 1e-6
ATTN_SCALE = HEAD_DIM ** -0.5
LOG2E = 1.4426950408889634
PAGE = 128
LANES = 128
VMEM_LIMIT = 56 * 1024 * 1024
PAGES_PER_STEP = 16
SEG_ALIGN = 8
SORT_CHUNK = 512
GROUP_UNROLL = 8

_dot = functools.partial(jnp.dot, preferred_element_type=F32)


def _dot_nt(a, b):
    return lax.dot_general(a, b, (((1,), (1,)), ((), ())), preferred_element_type=F32)


def _sigmoid(x):
    return 1.0 / (1.0 + jnp.exp(-x))


def _gelu(x):
    return 0.5 * x * (1.0 + jnp.tanh(0.7978845608028654 * (x + 0.044715 * (x * x * x))))


def _split3(x):
    hi = x.astype(BF16)
    r1 = x - hi.astype(F32)
    mid = r1.astype(BF16)
    lo = (r1 - mid.astype(F32)).astype(BF16)
    return hi, mid, lo


def _pack_halves(x):
    w = x.shape[1] // 2
    lo = lax.bitcast_convert_type(x[:, :w].astype(BF16).astype(F32), U32)
    hi = lax.bitcast_convert_type(x[:, w:].astype(BF16).astype(F32), U32)
    return (hi & jnp.uint32(0xFFFF0000)) | (lo >> 16)


def _unpack_halves(p):
    return (lax.bitcast_convert_type(p << 16, F32),
            lax.bitcast_convert_type(p & jnp.uint32(0xFFFF0000), F32))


def _dot3_left(m_bf16, x):
    hi, mid, lo = _split3(x)
    return _dot(m_bf16, hi) + _dot(m_bf16, mid) + _dot(m_bf16, lo)


def _dot3_right(x, m_bf16):
    hi, mid, lo = _split3(x)
    return _dot(hi, m_bf16) + _dot(mid, m_bf16) + _dot(lo, m_bf16)


def _params(*sem, row_dma=False):
    return pltpu.CompilerParams(dimension_semantics=sem, vmem_limit_bytes=VMEM_LIMIT,
                                disable_bounds_checks=row_dma)


def _const_spec(shape):
    zeros = (0,) * len(shape)
    return pl.BlockSpec(shape, lambda *_: zeros)


def _ada_kernel(c_ref, w_ref, b_ref, o_ref):
    c = c_ref[...]
    a = c * _sigmoid(c)
    o_ref[...] = jnp.dot(a, w_ref[...], preferred_element_type=F32,
                         precision=lax.Precision.HIGHEST) + b_ref[...]


def _ada(c, w_ada, b_ada):
    m, d = c.shape
    n = w_ada.shape[1]
    tn = 1024
    return pl.pallas_call(
        _ada_kernel,
        grid=(n // tn,),
        in_specs=[pl.BlockSpec((m, d), lambda j: (0, 0)),
                  pl.BlockSpec((d, tn), lambda j: (0, j)),
                  pl.BlockSpec((1, tn), lambda j: (0, j))],
        out_specs=pl.BlockSpec((m, tn), lambda j: (0, j)),
        out_shape=jax.ShapeDtypeStruct((m, n), F32),
        compiler_params=_params("arbitrary"),
        name="ada",
    )(c, w_ada, b_ada.reshape(1, n))


def _mix_in_kernel(x_ref, mod_ref, gmix_ref, wq_ref, wk_ref, wv_ref, wf_ref, wu_ref, wvg_ref,
                   wga_ref, wgb_ref, bf_ref, gq_ref, gk_ref, gvn_ref, bvn_ref, bd_ref,
                   wsp_ref, bsp_ref, wbb_ref, ltri_ref,
                   q_ref, kf_ref, vf_ref, kb_ref, vb_ref, lf_ref, fc_ref, mb_ref, sga_ref, vn_ref,
                   carry_sc):
    tl = x_ref.shape[1]
    x = x_ref[0]
    shift1 = mod_ref[0, :, 0:D_MODEL]
    scale1 = mod_ref[0, :, D_MODEL:2 * D_MODEL]
    ms = jnp.mean(x * x, axis=-1, keepdims=True)
    h = x * lax.rsqrt(ms + NORM_EPS) * gmix_ref[...] * (1.0 + scale1) + shift1
    hb = h.astype(BF16)
    bd = bd_ref[...]

    def head_norm(z, g):
        ss = _dot((z * z).astype(BF16), bd) * (1.0 / HEAD_DIM)
        return z * lax.rsqrt(ss + NORM_EPS) * g

    qn = head_norm(_dot(hb, wq_ref[...]), gq_ref[...]) * (ATTN_SCALE * LOG2E)
    q_ref[0] = qn.astype(BF16)
    kn = head_norm(_dot(hb, wk_ref[...]), gk_ref[...])
    kf_ref[0] = kn
    kb_ref[0] = kn.astype(BF16)
    v = _dot(hb, wv_ref[...])
    vf_ref[0] = v
    vb_ref[0] = v.astype(BF16)

    zf = _dot(hb, wf_ref[...]) + bf_ref[...]
    lf = jnp.minimum(zf, 0.0) - jnp.log(1.0 + jnp.exp(-jnp.abs(zf)))
    lf_ref[0] = lf[:, :N_HEADS]

    @pl.when(pl.program_id(1) == 0)
    def _():
        carry_sc[...] = jnp.zeros_like(carry_sc)

    fc = _dot3_left(ltri_ref[...], lf) + carry_sc[...]
    fc_ref[0] = fc[:, :N_HEADS]
    carry_sc[...] = fc[tl - 1:tl, :]

    gu = _gelu(_dot(hb, wu_ref[...]))
    gv = _gelu(_dot(hb, wvg_ref[...]))
    mu = jnp.mean(gv, axis=-1, keepdims=True)
    gc = gv - mu
    var = jnp.mean(gc * gc, axis=-1, keepdims=True)
    vn = gc * lax.rsqrt(var + NORM_EPS) * gvn_ref[...] + bvn_ref[...]
    vn_ref[0] = vn
    vnb = vn.astype(BF16)
    lane = lax.broadcasted_iota(I32, (CHUNK, LANES), 1)
    low = lane < (LANES // 2)
    zero = jnp.zeros((CHUNK, LANES), BF16)
    rows = []
    for c in range(tl // CHUNK):
        pieces = []
        for j in range(SGU_WIDTH // LANES):
            vp = vnb[c * CHUNK:(c + 1) * CHUNK, j * LANES:(j + 1) * LANES]
            mixed = (_dot(wsp_ref[2 * j], jnp.where(low, vp, zero))
                     + _dot(wsp_ref[2 * j + 1], jnp.where(low, zero, vp)) + bsp_ref[j])
            pieces.append(mixed)
        rows.append(jnp.concatenate(pieces, axis=1))
    mixed = rows[0] if len(rows) == 1 else jnp.concatenate(rows, axis=0)
    ob = (gu * mixed).astype(BF16)
    mb = _sigmoid(_dot(hb, wgb_ref[...])) * _dot(ob, wbb_ref[...])
    mb_ref[0] = mb.astype(BF16)
    sga_ref[0] = _sigmoid(_dot(hb, wga_ref[...])).astype(BF16)


def _mix_in(x, mod, wts, tl):
    bx, l, d = x.shape
    tlm = tl if mod.shape[1] > 1 else 1
    grid = (bx, l // tl)
    row = lambda b, i: (b, i, 0)
    mod_map = row if tlm > 1 else (lambda b, i: (b, 0, 0))
    names = ("gmix", "wq", "wk", "wv", "wf", "wu", "wvg", "wga", "wgb", "bf", "gq", "gk", "gvn",
             "bvn", "bd", "wsp", "bsp", "wbb", "ltri")
    consts = [wts[n] for n in names]
    out_widths = [(ATTN_WIDTH, BF16), (ATTN_WIDTH, F32), (ATTN_WIDTH, F32), (ATTN_WIDTH, BF16),
                  (ATTN_WIDTH, BF16), (N_HEADS, F32), (N_HEADS, F32), (D_MODEL, BF16),
                  (D_MODEL, BF16), (SGU_WIDTH, F32)]
    return pl.pallas_call(
        _mix_in_kernel,
        grid=grid,
        in_specs=[pl.BlockSpec((1, tl, d), row), pl.BlockSpec((1, tlm, 6 * d), mod_map)]
                 + [_const_spec(c.shape) for c in consts],
        out_specs=[pl.BlockSpec((1, tl, w), row) for w, _ in out_widths],
        out_shape=[jax.ShapeDtypeStruct((bx, l, w), dt) for w, dt in out_widths],
        scratch_shapes=[pltpu.VMEM((1, LANES), F32)],
        compiler_params=_params("arbitrary", "arbitrary"),
        name="mix_in",
    )(x, mod, *consts)


def _attn_kernel(qi_ref, ki_ref, q_ref, k_ref, v_ref, fq_ref, fk_ref, o_ref, m_sc, l_sc, acc_sc):
    tq = q_ref.shape[1]
    tk = k_ref.shape[1]
    pair = pl.program_id(1)
    qi = qi_ref[pair]
    ki = ki_ref[pair]

    @pl.when(ki == 0)
    def _():
        m_sc[...] = jnp.full_like(m_sc, -jnp.inf)
        l_sc[...] = jnp.zeros_like(l_sc)
        acc_sc[...] = jnp.zeros_like(acc_sc)

    lane = lax.broadcasted_iota(I32, (tq, LANES), 1)
    low = lane < HEAD_DIM

    def step(masked):
        ones = jnp.ones((tk, LANES), BF16)
        if masked:
            causal = (lax.broadcasted_iota(I32, (tq, tk), 0) >= lax.broadcasted_iota(I32, (tq, tk), 1))
        for j in range(ATTN_WIDTH // LANES):
            sl = slice(j * LANES, (j + 1) * LANES)
            qp = q_ref[0, :, sl]
            kp = k_ref[0, :, sl]
            vx = jnp.concatenate([v_ref[0, :, sl], ones], axis=1)
            zero = jnp.zeros_like(qp)
            alphas = []
            pvs = []
            for t in range(2):
                hd = 2 * j + t
                qh = jnp.where(low, qp, zero) if t == 0 else jnp.where(low, zero, qp)
                decay = (fq_ref[0, hd:hd + 1, 0:1] - fk_ref[0, hd:hd + 1, :]) * LOG2E
                s = _dot_nt(qh, kp) + decay
                if masked:
                    s = jnp.where(causal, s, -jnp.inf)
                m_prev = m_sc[hd]
                m_new = jnp.maximum(m_prev, jnp.max(s, axis=-1, keepdims=True))
                alpha = jnp.exp2(m_prev - m_new)
                p = jnp.concatenate([jnp.exp2(s[:, c * LANES:(c + 1) * LANES] - m_new)
                                     for c in range(tk // LANES)], axis=1)
                pv = _dot(p.astype(BF16), vx)
                l_sc[hd] = alpha * l_sc[hd] + pv[:, LANES:]
                m_sc[hd] = m_new
                alphas.append(alpha)
                pvs.append(pv[:, :LANES])
            acc_sc[j] = (acc_sc[j] * jnp.where(low, alphas[0], alphas[1])
                         + jnp.where(low, pvs[0], pvs[1]))

    @pl.when(ki < qi)
    def _():
        step(False)

    @pl.when(ki == qi)
    def _():
        step(True)
        for j in range(ATTN_WIDTH // LANES):
            inv = jnp.where(low, 1.0 / l_sc[2 * j], 1.0 / l_sc[2 * j + 1])
            o_ref[0, :, j * LANES:(j + 1) * LANES] = (acc_sc[j] * inv).astype(o_ref.dtype)


def _attn_prompt(q, k, v, fr, tq):
    b, s, w = q.shape
    nq = s // tq
    pairs = [(qi, ki) for qi in range(nq) for ki in range(qi + 1)]
    qi_arr = jnp.asarray([p[0] for p in pairs], I32)
    ki_arr = jnp.asarray([p[1] for p in pairs], I32)
    qmap = lambda bi, p, qa, ka: (bi, qa[p], 0)
    kmap = lambda bi, p, qa, ka: (bi, ka[p], 0)
    grid_spec = pltpu.PrefetchScalarGridSpec(
        num_scalar_prefetch=2,
        grid=(b, len(pairs)),
        in_specs=[pl.BlockSpec((1, tq, w), qmap),
                  pl.BlockSpec((1, tq, w), kmap),
                  pl.BlockSpec((1, tq, w), kmap),
                  pl.BlockSpec((1, N_HEADS, tq), lambda bi, p, qa, ka: (bi, 0, qa[p])),
                  pl.BlockSpec((1, N_HEADS, tq), lambda bi, p, qa, ka: (bi, 0, ka[p]))],
        out_specs=pl.BlockSpec((1, tq, w), qmap),
        scratch_shapes=[pltpu.VMEM((N_HEADS, tq, LANES), F32), pltpu.VMEM((N_HEADS, tq, LANES), F32),
                        pltpu.VMEM((w // LANES, tq, LANES), F32)],
    )
    return pl.pallas_call(
        _attn_kernel,
        grid_spec=grid_spec,
        out_shape=jax.ShapeDtypeStruct((b, s, w), BF16),
        compiler_params=_params("arbitrary", "arbitrary"),
        name="attn_prompt",
    )(qi_arr, ki_arr, q, k, v, fr, fr)


def _rows_to_tile(rows):
    sub = lax.broadcasted_iota(I32, (N_HEADS, PAGE), 0)
    tile = jnp.zeros((N_HEADS, PAGE), F32)
    for h, r in enumerate(rows):
        tile = jnp.where(sub == h, r, tile)
    return tile


def _attn_decode_kernel(pt_ref, q_ref, kn_ref, vn_ref, lfn_ref, u_ref, *refs):
    n = PAGES_PER_STEP
    lf_refs, k_refs, v_refs = refs[:n], refs[n:2 * n], refs[2 * n:3 * n]
    o_ref, m_sc, l_sc, acc_sc, carry_sc = refs[3 * n:]
    j = pl.program_id(1)
    nj = pl.num_programs(1)
    heads = range(N_HEADS)

    @pl.when(j == 0)
    def _():
        m_sc[...] = jnp.full_like(m_sc, -jnp.inf)
        l_sc[...] = jnp.zeros_like(l_sc)
        acc_sc[...] = jnp.zeros_like(acc_sc)
        carry_sc[...] = jnp.zeros_like(carry_sc)

    lfn = lfn_ref[0]
    lf_all = jnp.concatenate([lf_refs[i][0, 0] for i in range(n)], axis=0)
    later_all = _dot3_right(lf_all, u_ref[...])
    carry = carry_sc[...]
    decays = []
    for i in range(n):
        sl = slice(i * N_HEADS, (i + 1) * N_HEADS)
        decays.append((later_all[sl] + carry + lfn) * LOG2E)
        carry = carry + jnp.sum(lf_all[sl], axis=-1, keepdims=True)
    carry_sc[...] = carry

    rows = [[None] * N_HEADS for _ in range(n)]
    for h in heads:
        qh = q_ref[0, h]
        for i in range(n):
            rows[i][h] = jnp.sum(k_refs[i][0, 0, h] * qh, axis=0, keepdims=True)
    s = [_rows_to_tile(rows[i]) + decays[i] for i in range(n)]
    m = m_sc[...]
    m_new = m
    for i in range(n):
        m_new = jnp.maximum(m_new, jnp.max(s[i], axis=-1, keepdims=True))
    alpha = jnp.exp2(m - m_new)
    p = [jnp.exp2(s[i] - m_new) for i in range(n)]
    l = alpha * l_sc[...]
    for i in range(n):
        l = l + jnp.sum(p[i], axis=-1, keepdims=True)
    m_sc[...] = m_new
    l_sc[...] = l
    alpha_rep = jnp.broadcast_to(alpha, (N_HEADS, PAGE))
    for h in heads:
        a = acc_sc[h] * alpha_rep[h:h + 1, :]
        for i in range(n):
            a = a + p[i][h:h + 1, :] * v_refs[i][0, 0, h]
        acc_sc[h] = a

    @pl.when(j == nj - 1)
    def _():
        s_n = _rows_to_tile([jnp.sum(q_ref[0, h] * kn_ref[0, h], axis=0, keepdims=True) for h in heads])
        m_rep = jnp.broadcast_to(m_new, (N_HEADS, PAGE))
        m_fin = jnp.maximum(m_rep, s_n)
        a_fin = jnp.exp2(m_rep - m_fin)
        p_n = jnp.exp2(s_n - m_fin)
        inv = 1.0 / (a_fin * jnp.broadcast_to(l, (N_HEADS, PAGE)) + p_n)
        for h in heads:
            total = jnp.broadcast_to(jnp.sum(acc_sc[h], axis=-1, keepdims=True), (HEAD_DIM, PAGE))
            o_ref[0, h] = (a_fin[h:h + 1, :] * total + p_n[h:h + 1, :] * vn_ref[0, h]) * inv[h:h + 1, :]


def _attn_decode(page_table, q_rep, k_new_rep, v_new_rep, lf_new, cache_lf_t, cache_k_t, cache_v_t):
    nb, n_pages = page_table.shape
    n = PAGES_PER_STEP
    tok3 = lambda b, j, pt: (b, 0, 0)
    tok4 = lambda b, j, pt: (b, 0, 0, 0)

    def page(i, rank):
        def index_map(b, j, pt):
            return (0, pt[b * n_pages + n_pages - 1 - (j * n + i)]) + (0,) * (rank - 2)
        return index_map

    lane = jnp.arange(PAGE)
    later = (lane[:, None] > lane[None, :]).astype(BF16)
    rep = pl.BlockSpec((1, N_HEADS, HEAD_DIM, PAGE), tok4)
    grid_spec = pltpu.PrefetchScalarGridSpec(
        num_scalar_prefetch=1,
        grid=(nb, n_pages // n),
        in_specs=[rep, rep, rep,
                  pl.BlockSpec((1, N_HEADS, 1), tok3),
                  pl.BlockSpec((PAGE, PAGE), lambda b, j, pt: (0, 0))]
                 + [pl.BlockSpec((1, 1, N_HEADS, PAGE), page(i, 4)) for i in range(n)]
                 + [pl.BlockSpec((1, 1, N_HEADS, HEAD_DIM, PAGE), page(i, 5)) for i in range(n)]
                 + [pl.BlockSpec((1, 1, N_HEADS, HEAD_DIM, PAGE), page(i, 5)) for i in range(n)],
        out_specs=rep,
        scratch_shapes=[pltpu.VMEM((N_HEADS, 1), F32), pltpu.VMEM((N_HEADS, 1), F32),
                        pltpu.VMEM((N_HEADS, HEAD_DIM, PAGE), F32), pltpu.VMEM((N_HEADS, 1), F32)],
    )
    return pl.pallas_call(
        _attn_decode_kernel,
        grid_spec=grid_spec,
        out_shape=jax.ShapeDtypeStruct((nb, N_HEADS, HEAD_DIM, PAGE), F32),
        compiler_params=_params("arbitrary", "arbitrary"),
        name="attn_decode",
    )(page_table.reshape(-1), q_rep, k_new_rep, v_new_rep, lf_new, later,
      *([cache_lf_t] * n), *([cache_k_t] * n), *([cache_v_t] * n))


def _mix_out_kernel(x_ref, oa_ref, sga_ref, mb_ref, mod_ref, wba_ref, wo_ref, gffn_ref,
                    wrh_ref, wrl_ref, wsg_ref, wsu_ref, wsd_ref,
                    h2_ref, lg_ref, base_ref):
    d = D_MODEL
    x = x_ref[0]
    gate1 = mod_ref[0, :, 2 * d:3 * d]
    shift2 = mod_ref[0, :, 3 * d:4 * d]
    scale2 = mod_ref[0, :, 4 * d:5 * d]
    gate2 = mod_ref[0, :, 5 * d:6 * d]
    merged = sga_ref[0].astype(F32) * _dot(oa_ref[0], wba_ref[...]) + mb_ref[0].astype(F32)
    x1 = x + gate1 * _dot(merged.astype(BF16), wo_ref[...])
    ms = jnp.mean(x1 * x1, axis=-1, keepdims=True)
    h2 = x1 * lax.rsqrt(ms + NORM_EPS) * gffn_ref[...] * (1.0 + scale2) + shift2
    h2_ref[0] = _pack_halves(h2)
    hb = h2.astype(BF16)
    hl = (h2 - hb.astype(F32)).astype(BF16)
    lg_ref[0] = _dot(hb, wrh_ref[...]) + (_dot(hb, wrl_ref[...]) + _dot(hl, wrh_ref[...]))
    g = _dot(hb, wsg_ref[...])
    u = _dot(hb, wsu_ref[...])
    a = (g * _sigmoid(g) * u).astype(BF16)
    base_ref[0] = x1 + gate2 * _dot(a, wsd_ref[...])


def _mix_out(x, oa, sga, mb, mod, wts, tl):
    bx, l, d = x.shape
    tlm = tl if mod.shape[1] > 1 else 1
    row = lambda b, i: (b, i, 0)
    mod_map = row if tlm > 1 else (lambda b, i: (b, 0, 0))
    names = ("wba", "wo", "gffn", "wrh", "wrl", "wsg", "wsu", "wsd")
    consts = [wts[n] for n in names]
    return pl.pallas_call(
        _mix_out_kernel,
        grid=(bx, l // tl),
        in_specs=[pl.BlockSpec((1, tl, d), row), pl.BlockSpec((1, tl, ATTN_WIDTH), row),
                  pl.BlockSpec((1, tl, d), row), pl.BlockSpec((1, tl, d), row),
                  pl.BlockSpec((1, tlm, 6 * d), mod_map)] + [_const_spec(c.shape) for c in consts],
        out_specs=[pl.BlockSpec((1, tl, d // 2), row), pl.BlockSpec((1, tl, LANES), row),
                   pl.BlockSpec((1, tl, d), row)],
        out_shape=[jax.ShapeDtypeStruct((bx, l, d // 2), U32), jax.ShapeDtypeStruct((bx, l, LANES), F32),
                   jax.ShapeDtypeStruct((bx, l, d), F32)],
        compiler_params=_params("arbitrary", "arbitrary"),
        name="mix_out",
    )(x, oa, sga, mb, mod, *consts)


def _route_kernel(lg_ref, b_ref, before_ref, lower_ref, w_ref, r_ref, cnt_ref):
    tt = lg_ref.shape[1]
    epg = EXPERTS_PER_GROUP
    ninf = -jnp.inf
    iota = lax.broadcasted_iota(I32, (epg, tt), 0)
    sc = []
    biased = []
    gscore = []
    for g in range(N_EXPERT_GROUPS):
        s = _sigmoid(lg_ref[g * epg:(g + 1) * epg, :])
        bz = s + b_ref[g * epg:(g + 1) * epg, :]
        m1 = jnp.max(bz, axis=0, keepdims=True)
        first = jnp.min(jnp.where(bz == m1, iota, epg), axis=0, keepdims=True)
        m2 = jnp.max(jnp.where(iota == first, ninf, bz), axis=0, keepdims=True)
        sc.append(s)
        biased.append(bz)
        gscore.append(m1 + m2)
    cand = []
    for g in range(N_EXPERT_GROUPS):
        rank = jnp.zeros((1, tt), I32)
        for o in range(N_EXPERT_GROUPS):
            if o == g:
                continue
            beats = (gscore[o] >= gscore[g]) if o < g else (gscore[o] > gscore[g])
            rank = rank + beats.astype(I32)
        cand.append(jnp.where(rank < TOP_K_GROUPS, biased[g], ninf))
    ws = []
    picks = []
    for k in range(TOP_K):
        mx = cand[0]
        for g in range(1, N_EXPERT_GROUPS):
            mx = jnp.maximum(mx, cand[g])
        mx = jnp.max(mx, axis=0, keepdims=True)
        fi = jnp.where(cand[0] == mx, iota, N_EXPERTS)
        for g in range(1, N_EXPERT_GROUPS):
            fi = jnp.minimum(fi, jnp.where(cand[g] == mx, iota + g * epg, N_EXPERTS))
        fi = jnp.min(fi, axis=0, keepdims=True)
        wk = jnp.zeros((epg, tt), F32)
        for g in range(N_EXPERT_GROUPS):
            hit = (iota + g * epg) == fi
            wk = wk + jnp.where(hit, sc[g], 0.0)
            cand[g] = jnp.where(hit, ninf, cand[g])
        picks.append(fi)
        ws.append(jnp.sum(wk, axis=0, keepdims=True))
    tot = ws[0]
    for k in range(1, TOP_K):
        tot = tot + ws[k]
    for k in range(TOP_K):
        w_ref[k:k + 1, :] = ws[k] / tot * ROUTED_SCALE

    chosen = []
    for g in range(N_EXPERT_GROUPS):
        sel = jnp.zeros((epg, tt), F32)
        for k in range(TOP_K):
            sel = sel + jnp.where((iota + g * epg) == picks[k], 1.0, 0.0)
        chosen.append(sel)
    chosen = jnp.concatenate(chosen, axis=0)
    earlier = _dot(chosen.astype(BF16), before_ref[...])
    run = jnp.floor((jnp.sum(chosen, axis=1, keepdims=True) + (SEG_ALIGN - 1)) * (1.0 / SEG_ALIGN)) * SEG_ALIGN
    run_rep = jnp.broadcast_to(run, (N_EXPERTS, LANES))
    cnt_ref[0] = run_rep
    start = _dot(lower_ref[...], run_rep.astype(BF16))[:, 0:1]
    local = earlier + start
    for k in range(TOP_K):
        ck = jnp.zeros((epg, tt), F32)
        for g in range(N_EXPERT_GROUPS):
            ck = ck + jnp.where((iota + g * epg) == picks[k], local[g * epg:(g + 1) * epg, :], 0.0)
        r_ref[k:k + 1, :] = jnp.sum(ck, axis=0, keepdims=True).astype(I32)


def _route(logits_t, b_router, tt):
    t = logits_t.shape[1]
    r = jnp.arange(tt)
    before = (r[:, None] < r[None, :]).astype(BF16)
    e = jnp.arange(N_EXPERTS)
    lower = (e[None, :] < e[:, None]).astype(BF16)
    col = lambda i: (0, i)
    return pl.pallas_call(
        _route_kernel,
        grid=(t // tt,),
        in_specs=[pl.BlockSpec((N_EXPERTS, tt), col),
                  pl.BlockSpec((N_EXPERTS, 1), lambda i: (0, 0)),
                  pl.BlockSpec((tt, tt), lambda i: (0, 0)),
                  pl.BlockSpec((N_EXPERTS, N_EXPERTS), lambda i: (0, 0))],
        out_specs=[pl.BlockSpec((TOP_K, tt), col), pl.BlockSpec((TOP_K, tt), col),
                   pl.BlockSpec((1, N_EXPERTS, LANES), lambda i: (i, 0, 0))],
        out_shape=[jax.ShapeDtypeStruct((TOP_K, t), F32), jax.ShapeDtypeStruct((TOP_K, t), I32),
                   jax.ShapeDtypeStruct((t // tt, N_EXPERTS, LANES), F32)],
        compiler_params=_params("arbitrary"),
        name="route",
    )(logits_t, b_router.reshape(N_EXPERTS, 1), before, lower)


def _for_groups(count, fn):
    bulk = count // GROUP_UNROLL

    def many(q, c):
        for u in range(GROUP_UNROLL):
            fn(q * GROUP_UNROLL + u)
        return c

    def one(g, c):
        fn(g)
        return c

    lax.fori_loop(0, bulk, many, 0)
    lax.fori_loop(bulk * GROUP_UNROLL, count, one, 0)


def _local_sort_kernel(to_ref, r_ref, h_ref, xs_hbm, buf, sem):
    tt = h_ref.shape[0]
    h = jnp.concatenate(_unpack_halves(h_ref[...]), axis=1).astype(BF16)
    for c in range(buf.shape[0] // SORT_CHUNK):
        rows = c * SORT_CHUNK + lax.broadcasted_iota(I32, (SORT_CHUNK, tt), 0)
        hit = rows == r_ref[0:1, :]
        for k in range(1, TOP_K):
            hit = hit | (rows == r_ref[k:k + 1, :])
        place = jnp.where(hit, 1.0, 0.0).astype(BF16)
        buf[c * SORT_CHUNK:(c + 1) * SORT_CHUNK, :] = _pack_halves(_dot(place, h))

    def copy(g):
        return pltpu.make_async_copy(buf.at[pl.ds(pl.multiple_of(g * SEG_ALIGN, SEG_ALIGN), SEG_ALIGN)],
                                     xs_hbm.at[pl.ds(pl.multiple_of(to_ref[0, 0, g], SEG_ALIGN), SEG_ALIGN)],
                                     sem.at[0])

    used = to_ref[0, 0, buf.shape[0] // SEG_ALIGN]
    _for_groups(used, lambda g: copy(g).start())
    _for_groups(used, lambda g: copy(g).wait())


def _local_rows(tt):
    pad = N_EXPERTS * SEG_ALIGN
    return -(-(tt * TOP_K + pad) // SORT_CHUNK) * SORT_CHUNK


def _local_sort(to_expert, r_t, h2, tt, out_rows):
    t, dp = h2.shape
    rows = _local_rows(tt)
    n = t // tt
    return pl.pallas_call(
        _local_sort_kernel,
        grid=(n,),
        in_specs=[pl.BlockSpec((1, 1, rows // SEG_ALIGN + 1), lambda i: (i, 0, 0), memory_space=pltpu.SMEM),
                  pl.BlockSpec((TOP_K, tt), lambda i: (0, i)), pl.BlockSpec((tt, dp), lambda i: (i, 0))],
        out_specs=pl.BlockSpec(memory_space=pl.ANY),
        out_shape=jax.ShapeDtypeStruct((out_rows, dp), U32),
        scratch_shapes=[pltpu.VMEM((rows, dp), U32), pltpu.SemaphoreType.DMA((1,))],
        compiler_params=_params("arbitrary", row_dma=True),
        name="moe_local_sort",
    )(to_expert.reshape(n, 1, rows // SEG_ALIGN + 1), r_t, h2)


VISIT_ACTIVE = 1
VISIT_FIRST = 2


def _moe_kernel(vt_ref, ve_ref, vf_ref, gs_ref, ge_ref, x_ref, wg_ref, wu_ref, wd_ref, o_ref):
    v = pl.program_id(0)
    tm = x_ref.shape[0]
    flags = vf_ref[v]

    @pl.when(flags >= VISIT_ACTIVE)
    def _():
        e = ve_ref[v]
        x = jnp.concatenate(_unpack_halves(x_ref[...]), axis=1).astype(BF16)
        g = _dot(x, wg_ref[0].astype(BF16))
        u = _dot(x, wu_ref[0].astype(BF16))
        a = (g * _sigmoid(g) * u).astype(BF16)
        res = _pack_halves(_dot(a, wd_ref[0].astype(BF16)))
        row = vt_ref[v] * tm + lax.broadcasted_iota(I32, (tm, 1), 0)
        mine = (row >= gs_ref[e]) & (row < ge_ref[e])

        @pl.when(flags >= VISIT_FIRST)
        def _():
            o_ref[...] = jnp.where(mine, res, jnp.zeros_like(res))

        @pl.when(flags < VISIT_FIRST)
        def _():
            o_ref[...] = jnp.where(mine, res, o_ref[...])


def _moe(plan, xs, w_gate, w_up, w_down, tm):
    visit_tile, visit_e, visit_flags, gs, ge = plan
    n_visits = visit_tile.shape[0]
    rows, dp = xs.shape
    d = 2 * dp
    tile = lambda v, vt, ve, vf, s, e: (vt[v], 0)
    expert = lambda v, vt, ve, vf, s, e: (ve[v], 0, 0)
    grid_spec = pltpu.PrefetchScalarGridSpec(
        num_scalar_prefetch=5,
        grid=(n_visits,),
        in_specs=[pl.BlockSpec((tm, dp), tile),
                  pl.BlockSpec((1, d, D_EXPERT), expert),
                  pl.BlockSpec((1, d, D_EXPERT), expert),
                  pl.BlockSpec((1, D_EXPERT, d), expert)],
        out_specs=pl.BlockSpec((tm, dp), tile),
    )
    return pl.pallas_call(
        _moe_kernel,
        grid_spec=grid_spec,
        out_shape=jax.ShapeDtypeStruct((rows, dp), U32),
        compiler_params=_params("arbitrary"),
        name="moe_experts",
    )(visit_tile, visit_e, visit_flags, gs, ge, xs, w_gate, w_up, w_down)


def _local_combine_kernel(fr_ref, frn_ref, eo_hbm, r_ref, w_ref, base_ref, mod_ref, y_ref, buf, sem):
    tt = base_ref.shape[1]
    half = D_MODEL // 2
    i = pl.program_id(0) * pl.num_programs(1) + pl.program_id(1)
    n = pl.num_programs(0) * pl.num_programs(1)
    slot = i % 2
    groups = buf.shape[1] // SEG_ALIGN

    def copy(tab_ref, s, g):
        return pltpu.make_async_copy(eo_hbm.at[pl.ds(pl.multiple_of(tab_ref[0, 0, g], SEG_ALIGN), SEG_ALIGN)],
                                     buf.at[s, pl.ds(pl.multiple_of(g * SEG_ALIGN, SEG_ALIGN), SEG_ALIGN)],
                                     sem.at[s])

    def issue(tab_ref, s):
        _for_groups(tab_ref[0, 0, groups], lambda g: copy(tab_ref, s, g).start())

    @pl.when(i == 0)
    def _():
        buf[...] = jnp.zeros_like(buf)
        issue(fr_ref, 0)

    @pl.when(i + 1 < n)
    def _():
        issue(frn_ref, 1 - slot)

    _for_groups(fr_ref[0, 0, groups], lambda g: copy(fr_ref, slot, g).wait())

    e_ref = buf.at[slot]
    lo = jnp.zeros((tt, half), F32)
    hi = jnp.zeros((tt, half), F32)
    for c in range(e_ref.shape[0] // SORT_CHUNK):
        cols = c * SORT_CHUNK + lax.broadcasted_iota(I32, (tt, SORT_CHUNK), 1)
        wm = jnp.zeros((tt, SORT_CHUNK), F32)
        for k in range(TOP_K):
            wm = jnp.where(cols == r_ref[0, :, k:k + 1], w_ref[0, :, k:k + 1], wm)
        wh = wm.astype(BF16)
        wl = (wm - wh.astype(F32)).astype(BF16)
        e_lo, e_hi = _unpack_halves(e_ref[c * SORT_CHUNK:(c + 1) * SORT_CHUNK, :])
        e_lo, e_hi = e_lo.astype(BF16), e_hi.astype(BF16)
        lo = lo + (_dot(wh, e_lo) + _dot(wl, e_lo))
        hi = hi + (_dot(wh, e_hi) + _dot(wl, e_hi))
    gate2 = mod_ref[0, :, 5 * D_MODEL:6 * D_MODEL]
    y_ref[0, :, :half] = base_ref[0, :, :half] + gate2[:, :half] * lo
    y_ref[0, :, half:] = base_ref[0, :, half:] + gate2[:, half:] * hi


def _local_combine(from_expert, eo, r_tok, w_tok, base, mod, tt):
    bx, l, d = base.shape
    nl = l // tt
    n = bx * nl
    rows = _local_rows(tt)
    groups = rows // SEG_ALIGN
    tab = from_expert.reshape(n, 1, groups + 1)
    tlm = tt if mod.shape[1] > 1 else 1
    row = lambda b, i: (b, i, 0)
    mod_map = row if tlm > 1 else (lambda b, i: (b, 0, 0))
    return pl.pallas_call(
        _local_combine_kernel,
        grid=(bx, nl),
        in_specs=[pl.BlockSpec((1, 1, groups + 1), lambda b, i: (b * nl + i, 0, 0), memory_space=pltpu.SMEM),
                  pl.BlockSpec((1, 1, groups + 1), lambda b, i: (jnp.minimum(b * nl + i + 1, n - 1), 0, 0),
                               memory_space=pltpu.SMEM),
                  pl.BlockSpec(memory_space=pl.ANY),
                  pl.BlockSpec((1, tt, TOP_K), row), pl.BlockSpec((1, tt, TOP_K), row),
                  pl.BlockSpec((1, tt, d), row),
                  pl.BlockSpec((1, tlm, 6 * d), mod_map)],
        out_specs=pl.BlockSpec((1, tt, d), row),
        out_shape=jax.ShapeDtypeStruct((bx, l, d), F32),
        scratch_shapes=[pltpu.VMEM((2, rows, d // 2), U32), pltpu.SemaphoreType.DMA((2,))],
        compiler_params=_params("arbitrary", "arbitrary", row_dma=True),
        name="moe_local_combine",
    )(tab, tab, eo, r_tok, w_tok, base, mod)


def _moe_plan(runs, tt, tm):
    n_tok_tiles = runs.shape[0]
    runs = runs.astype(I32)
    local_start = jnp.cumsum(runs, axis=1) - runs
    totals = jnp.sum(runs, axis=0)
    ge = jnp.cumsum(totals)
    gs = ge - totals
    expert_major = gs[None, :] + jnp.cumsum(runs, axis=0) - runs
    local_row = jnp.arange(_local_rows(tt) // SEG_ALIGN, dtype=I32) * SEG_ALIGN
    run_of = jnp.sum((local_start + runs)[:, None, :] <= local_row[None, :, None], axis=2)
    in_run = run_of[:, :, None] == jnp.arange(N_EXPERTS, dtype=I32)[None, None, :]
    shift = jnp.sum(jnp.where(in_run, (expert_major - local_start)[:, None, :], 0), axis=2)
    table = jnp.where(run_of < N_EXPERTS, local_row[None, :] + shift, -1).astype(I32)
    table = jnp.concatenate([table, jnp.sum(runs, axis=1, keepdims=True) // SEG_ALIGN], axis=1)
    max_rows = -(-(n_tok_tiles * (tt * TOP_K + N_EXPERTS * (SEG_ALIGN - 1))) // tm) * tm
    n_tiles = max_rows // tm
    n_visits = n_tiles + N_EXPERTS
    experts = jnp.arange(N_EXPERTS, dtype=I32)
    first_tile = gs // tm
    n_vis = jnp.where(totals > 0, (ge - 1) // tm - first_tile + 1, 0)
    vend = jnp.cumsum(n_vis)
    vstart = vend - n_vis
    v = jnp.arange(n_visits, dtype=I32)
    active = v < vend[-1]
    e_of_v = jnp.minimum(jnp.sum(vend[None, :] <= v[:, None], axis=1), N_EXPERTS - 1).astype(I32)
    onehot = e_of_v[:, None] == experts[None, :]
    pick = lambda table: jnp.sum(jnp.where(onehot, table[None, :], 0), axis=1)
    last_tile = jnp.maximum(ge[-1] - 1, 0) // tm
    tile_of_v = jnp.where(active, pick(first_tile) + v - pick(vstart), last_tile).astype(I32)
    prev_tile = jnp.concatenate([jnp.full((1,), -1, I32), tile_of_v[:-1]])
    flags = jnp.where(active, VISIT_ACTIVE + VISIT_FIRST * (tile_of_v != prev_tile), 0).astype(I32)
    return table, (tile_of_v, e_of_v, flags, gs, ge), max_rows


def _prep_weights(w_in, b_forget, g_q, g_k, g_vnorm, b_vnorm, g_norm_mix, g_norm_ffn,
                  w_branch_a, w_branch_b, w_out, w_router, w_sh_gate, w_sh_up, w_sh_down):
    aw, sw, d = ATTN_WIDTH, SGU_WIDTH, D_MODEL
    o = 3 * aw + N_HEADS
    wf = jnp.zeros((d, LANES), F32).at[:, :N_HEADS].set(w_in[:, 3 * aw:o])
    bf = jnp.zeros((1, LANES), F32).at[0, :N_HEADS].set(b_forget)
    lane = jnp.arange(aw)
    bd = (lane[:, None] // HEAD_DIM == lane[None, :] // HEAD_DIM).astype(BF16)
    wr = jnp.zeros((d, LANES), F32).at[:, :N_EXPERTS].set(w_router)
    wrh = wr.astype(BF16)
    wrl = (wr - wrh.astype(F32)).astype(BF16)
    return dict(
        gmix=g_norm_mix.reshape(1, d), gffn=g_norm_ffn.reshape(1, d),
        wq=w_in[:, 0:aw].astype(BF16), wk=w_in[:, aw:2 * aw].astype(BF16),
        wv=w_in[:, 2 * aw:3 * aw].astype(BF16), wf=wf.astype(BF16), bf=bf,
        wu=w_in[:, o:o + sw].astype(BF16), wvg=w_in[:, o + sw:o + 2 * sw].astype(BF16),
        wga=w_in[:, o + 2 * sw:o + 2 * sw + d].astype(BF16),
        wgb=w_in[:, o + 2 * sw + d:o + 2 * sw + 2 * d].astype(BF16),
        gq=jnp.tile(g_q, N_HEADS).reshape(1, aw), gk=jnp.tile(g_k, N_HEADS).reshape(1, aw),
        gvn=g_vnorm.reshape(1, sw), bvn=b_vnorm.reshape(1, sw), bd=bd,
        wbb=w_branch_b.astype(BF16), wba=w_branch_a.astype(BF16), wo=w_out.astype(BF16),
        wrh=wrh, wrl=wrl, wsg=w_sh_gate.astype(BF16), wsu=w_sh_up.astype(BF16),
        wsd=w_sh_down.astype(BF16))


def _spatial_weights(w_spatial, b_spatial, rows_are_sequences, tl):
    if rows_are_sequences:
        wsp = w_spatial[:, 0, 0][:, None, None] * jnp.eye(CHUNK, dtype=F32)[None]
        b = jnp.broadcast_to(b_spatial[:, 0:1], (SGU_GROUPS, CHUNK))
    else:
        wsp = jnp.where(jnp.tril(jnp.ones((CHUNK, CHUNK), bool)), w_spatial, 0)
        b = b_spatial
    half = LANES // 2
    bsp = jnp.repeat(b.reshape(SGU_GROUPS // 2, 2, CHUNK), half, axis=1)
    bsp = bsp.transpose(0, 2, 1)
    r = jnp.arange(tl)
    ltri = (r[:, None] >= r[None, :]).astype(BF16)
    return dict(wsp=wsp.astype(BF16), bsp=bsp, ltri=ltri)


def _layer(x, mod, attend, wts, w_spatial, b_spatial, b_router, w_exp_gate, w_exp_up, w_exp_down,
           rows_are_sequences, tl, tm, tt):
    bx, l, d = x.shape
    wts = dict(wts, **_spatial_weights(w_spatial, b_spatial, rows_are_sequences, tl))
    q, kf, vf, kb, vb, lf, fc, mb, sga, vn = _mix_in(x, mod, wts, tl)
    oa = attend(q, kf, vf, kb, vb, lf, fc)
    h2, logits, base = _mix_out(x, oa, sga, mb, mod, wts, tl)
    t = bx * l
    w_t, r_t, runs = _route(logits.reshape(t, LANES).T, b_router, tt)
    table, plan, max_rows = _moe_plan(runs[:, :, 0], tt, tm)
    xs = _local_sort(table, r_t, h2.reshape(t, d // 2), tt, max_rows)
    eo = _moe(plan, xs, w_exp_gate, w_exp_up, w_exp_down, tm)
    y = _local_combine(table, eo, r_t.T.reshape(bx, l, TOP_K), w_t.T.reshape(bx, l, TOP_K), base, mod, tt)
    return y, kf, vf, lf, vn


def kernel(x_prompt, x_sample, c_prompt, c_sample, cache_k, cache_v, cache_logf, page_table, w_ada, b_ada, g_norm_mix, g_norm_ffn, w_in, b_forget, g_q, g_k, g_vnorm, b_vnorm, w_spatial, b_spatial, w_branch_a, w_branch_b, w_out, w_router, b_router, w_exp_gate, w_exp_up, w_exp_down, w_sh_gate, w_sh_up, w_sh_down):
    assert w_ada.shape[0] == 1, "one layer"
    b, s, d = x_prompt.shape
    nb = x_sample.shape[0]

    c_all = jnp.concatenate([c_prompt, c_sample], axis=0)
    pad = (-c_all.shape[0]) % 8
    c_all = jnp.pad(c_all, ((0, pad), (0, 0)))
    mod_all = _ada(c_all, w_ada[0], b_ada[0])
    mod_p = mod_all[:b].reshape(b, 1, 6 * d)
    mod_s = mod_all[b:b + nb].reshape(1, nb, 6 * d)

    wts = _prep_weights(w_in[0], b_forget[0], g_q[0], g_k[0], g_vnorm[0], b_vnorm[0], g_norm_mix[0],
                        g_norm_ffn[0], w_branch_a[0], w_branch_b[0], w_out[0], w_router[0],
                        w_sh_gate[0], w_sh_up[0], w_sh_down[0])
    experts = (w_exp_gate[0], w_exp_up[0], w_exp_down[0])

    tq = min(512, s)

    def attend_prompt(q, kf, vf, kb, vb, lf, fc):
        return _attn_prompt(q, kb, vb, fc.transpose(0, 2, 1), tq)

    def attend_sample(q, kf, vf, kb, vb, lf, fc):
        def lane_rep(a):
            a = a.astype(F32).reshape(nb, N_HEADS, HEAD_DIM, 1)
            return jnp.broadcast_to(a, (nb, N_HEADS, HEAD_DIM, PAGE))

        o = _attn_decode(page_table, lane_rep(q), lane_rep(kf), lane_rep(vf),
                         lf.reshape(nb, N_HEADS, 1),
                         cache_logf.transpose(0, 1, 3, 2),
                         cache_k.transpose(0, 1, 3, 4, 2),
                         cache_v.transpose(0, 1, 3, 4, 2))
        return o[..., 0].reshape(1, nb, ATTN_WIDTH).astype(BF16)

    y_s, k_s, v_s, lf_s, vn_s = _layer(x_sample.reshape(1, nb, d), mod_s, attend_sample, wts,
                                       w_spatial[0], b_spatial[0], b_router[0], *experts,
                                       rows_are_sequences=True, tl=nb, tm=32, tt=nb)
    y_p, k_p, v_p, lf_p, _ = _layer(x_prompt, mod_p, attend_prompt, wts, w_spatial[0], b_spatial[0],
                                    b_router[0], *experts, rows_are_sequences=False,
                                    tl=min(512, s), tm=min(512, s), tt=min(256, s))
    hd5 = (1, b, s, N_HEADS, HEAD_DIM)
    sd5 = (1, nb, 1, N_HEADS, HEAD_DIM)
    return (y_p, y_s.reshape(nb, 1, d),
            k_p.reshape(hd5), v_p.reshape(hd5), lf_p.reshape(1, b, s, N_HEADS),
            k_s.reshape(sd5), v_s.reshape(sd5), lf_s.reshape(1, nb, 1, N_HEADS),
            vn_s.reshape(1, nb, 1, SGU_WIDTH))
```

```python
import functools

import jax
import jax.numpy as jnp
from jax import lax
from jax.experimental import pallas as pl
from jax.experimental.pallas import tpu as pltpu

F32 = jnp.float32
BF16 = jnp.bfloat16
I32 = jnp.int32
U32 = jnp.uint32

D_MODEL = 1024
N_HEADS = 8
HEAD_DIM = 64
ATTN_WIDTH = N_HEADS * HEAD_DIM
SGU_GROUPS = 8
SGU_WIDTH = 512
CHUNK = 128
N_EXPERTS = 64
TOP_K = 8
N_EXPERT_GROUPS = 8
TOP_K_GROUPS = 4
EXPERTS_PER_GROUP = 8
D_EXPERT = 256
D_SHARED = 256
ROUTED_SCALE = 2.5
NORM_EPS = 1e-6
ATTN_SCALE = HEAD_DIM ** -0.5
LOG2E = 1.4426950408889634
PAGE = 128
LANES = 128
VMEM_LIMIT = 56 * 1024 * 1024
PAGES_PER_STEP = 32
SEG_ALIGN = 8
SORT_CHUNK = 512
GROUP_UNROLL = 16

_dot = functools.partial(jnp.dot, preferred_element_type=F32)


def _dot_nt(a, b):
    return lax.dot_general(a, b, (((1,), (1,)), ((), ())), preferred_element_type=F32)


def _sigmoid(x):
    return 1.0 / (1.0 + jnp.exp(-x))


def _gelu(x):
    return 0.5 * x * (1.0 + jnp.tanh(0.7978845608028654 * (x + 0.044715 * (x * x * x))))


def _split3(x):
    hi = x.astype(BF16)
    r1 = x - hi.astype(F32)
    mid = r1.astype(BF16)
    lo = (r1 - mid.astype(F32)).astype(BF16)
    return hi, mid, lo


def _pack_halves(x):
    w = x.shape[1] // 2
    lo = lax.bitcast_convert_type(x[:, :w].astype(BF16).astype(F32), U32)
    hi = lax.bitcast_convert_type(x[:, w:].astype(BF16).astype(F32), U32)
    return (hi & jnp.uint32(0xFFFF0000)) | (lo >> 16)


def _unpack_halves(p):
    return (lax.bitcast_convert_type(p << 16, F32),
            lax.bitcast_convert_type(p & jnp.uint32(0xFFFF0000), F32))


def _dot3_left(m_bf16, x):
    hi, mid, lo = _split3(x)
    return _dot(m_bf16, hi) + _dot(m_bf16, mid) + _dot(m_bf16, lo)


def _dot3_right(x, m_bf16):
    hi, mid, lo = _split3(x)
    return _dot(hi, m_bf16) + _dot(mid, m_bf16) + _dot(lo, m_bf16)


def _params(*sem, row_dma=False):
    return pltpu.CompilerParams(dimension_semantics=sem, vmem_limit_bytes=VMEM_LIMIT,
                                disable_bounds_checks=row_dma)


def _const_spec(shape):
    zeros = (0,) * len(shape)
    return pl.BlockSpec(shape, lambda *_: zeros)


def _ada_kernel(c_ref, w_ref, b_ref, o_ref):
    c = c_ref[...]
    a = c * _sigmoid(c)
    o_ref[...] = jnp.dot(a, w_ref[...], preferred_element_type=F32,
                         precision=lax.Precision.HIGHEST) + b_ref[...]


def _ada(c, w_ada, b_ada):
    m, d = c.shape
    n = w_ada.shape[1]
    tn = 1024
    return pl.pallas_call(
        _ada_kernel,
        grid=(n // tn,),
        in_specs=[pl.BlockSpec((m, d), lambda j: (0, 0)),
                  pl.BlockSpec((d, tn), lambda j: (0, j)),
                  pl.BlockSpec((1, tn), lambda j: (0, j))],
        out_specs=pl.BlockSpec((m, tn), lambda j: (0, j)),
        out_shape=jax.ShapeDtypeStruct((m, n), F32),
        compiler_params=_params("arbitrary"),
        name="ada",
    )(c, w_ada, b_ada.reshape(1, n))


def _mix_in_kernel(x_ref, mod_ref, gmix_ref, wq_ref, wk_ref, wv_ref, wf_ref, wu_ref, wvg_ref,
                   wga_ref, wgb_ref, bf_ref, gq_ref, gk_ref, gvn_ref, bvn_ref, bd_ref,
                   wsp_ref, bsp_ref, wbb_ref, ltri_ref,
                   q_ref, kf_ref, vf_ref, kb_ref, vb_ref, lf_ref, fc_ref, mb_ref, sga_ref, vn_ref,
                   carry_sc):
    tl = x_ref.shape[1]
    x = x_ref[0]
    shift1 = mod_ref[0, :, 0:D_MODEL]
    scale1 = mod_ref[0, :, D_MODEL:2 * D_MODEL]
    ms = jnp.mean(x * x, axis=-1, keepdims=True)
    h = x * lax.rsqrt(ms + NORM_EPS) * gmix_ref[...] * (1.0 + scale1) + shift1
    hb = h.astype(BF16)
    bd = bd_ref[...]

    def head_norm(z, g):
        ss = _dot((z * z).astype(BF16), bd) * (1.0 / HEAD_DIM)
        return z * lax.rsqrt(ss + NORM_EPS) * g

    qn = head_norm(_dot(hb, wq_ref[...]), gq_ref[...]) * (ATTN_SCALE * LOG2E)
    q_ref[0] = qn.astype(BF16)
    kn = head_norm(_dot(hb, wk_ref[...]), gk_ref[...])
    kf_ref[0] = kn
    kb_ref[0] = kn.astype(BF16)
    v = _dot(hb, wv_ref[...])
    vf_ref[0] = v
    vb_ref[0] = v.astype(BF16)

    zf = _dot(hb, wf_ref[...]) + bf_ref[...]
    lf = jnp.minimum(zf, 0.0) - jnp.log(1.0 + jnp.exp(-jnp.abs(zf)))
    lf_ref[0] = lf[:, :N_HEADS]

    @pl.when(pl.program_id(1) == 0)
    def _():
        carry_sc[...] = jnp.zeros_like(carry_sc)

    fc = _dot3_left(ltri_ref[...], lf) + carry_sc[...]
    fc_ref[0] = fc[:, :N_HEADS]
    carry_sc[...] = fc[tl - 1:tl, :]

    gu = _gelu(_dot(hb, wu_ref[...]))
    gv = _gelu(_dot(hb, wvg_ref[...]))
    mu = jnp.mean(gv, axis=-1, keepdims=True)
    gc = gv - mu
    var = jnp.mean(gc * gc, axis=-1, keepdims=True)
    vn = gc * lax.rsqrt(var + NORM_EPS) * gvn_ref[...] + bvn_ref[...]
    vn_ref[0] = vn
    vnb = vn.astype(BF16)
    lane = lax.broadcasted_iota(I32, (CHUNK, LANES), 1)
    low = lane < (LANES // 2)
    zero = jnp.zeros((CHUNK, LANES), BF16)
    rows = []
    for c in range(tl // CHUNK):
        pieces = []
        for j in range(SGU_WIDTH // LANES):
            vp = vnb[c * CHUNK:(c + 1) * CHUNK, j * LANES:(j + 1) * LANES]
            mixed = (_dot(wsp_ref[2 * j], jnp.where(low, vp, zero))
                     + _dot(wsp_ref[2 * j + 1], jnp.where(low, zero, vp)) + bsp_ref[j])
            pieces.append(mixed)
        rows.append(jnp.concatenate(pieces, axis=1))
    mixed = rows[0] if len(rows) == 1 else jnp.concatenate(rows, axis=0)
    ob = (gu * mixed).astype(BF16)
    mb = _sigmoid(_dot(hb, wgb_ref[...])) * _dot(ob, wbb_ref[...])
    mb_ref[0] = mb.astype(BF16)
    sga_ref[0] = _sigmoid(_dot(hb, wga_ref[...])).astype(BF16)


def _mix_in(x, mod, wts, tl):
    bx, l, d = x.shape
    tlm = tl if mod.shape[1] > 1 else 1
    grid = (bx, l // tl)
    row = lambda b, i: (b, i, 0)
    mod_map = row if tlm > 1 else (lambda b, i: (b, 0, 0))
    names = ("gmix", "wq", "wk", "wv", "wf", "wu", "wvg", "wga", "wgb", "bf", "gq", "gk", "gvn",
             "bvn", "bd", "wsp", "bsp", "wbb", "ltri")
    consts = [wts[n] for n in names]
    out_widths = [(ATTN_WIDTH, BF16), (ATTN_WIDTH, F32), (ATTN_WIDTH, F32), (ATTN_WIDTH, BF16),
                  (ATTN_WIDTH, BF16), (N_HEADS, F32), (N_HEADS, F32), (D_MODEL, BF16),
                  (D_MODEL, BF16), (SGU_WIDTH, F32)]
    return pl.pallas_call(
        _mix_in_kernel,
        grid=grid,
        in_specs=[pl.BlockSpec((1, tl, d), row), pl.BlockSpec((1, tlm, 6 * d), mod_map)]
                 + [_const_spec(c.shape) for c in consts],
        out_specs=[pl.BlockSpec((1, tl, w), row) for w, _ in out_widths],
        out_shape=[jax.ShapeDtypeStruct((bx, l, w), dt) for w, dt in out_widths],
        scratch_shapes=[pltpu.VMEM((1, LANES), F32)],
        compiler_params=_params("arbitrary", "arbitrary"),
        name="mix_in",
    )(x, mod, *consts)


def _attn_kernel(qi_ref, ki_ref, q_ref, k_ref, v_ref, fq_ref, fk_ref, o_ref, m_sc, l_sc, acc_sc):
    tq = q_ref.shape[1]
    tk = k_ref.shape[1]
    pair = pl.program_id(1)
    qi = qi_ref[pair]
    ki = ki_ref[pair]

    @pl.when(ki == 0)
    def _():
        m_sc[...] = jnp.full_like(m_sc, -jnp.inf)
        l_sc[...] = jnp.zeros_like(l_sc)
        acc_sc[...] = jnp.zeros_like(acc_sc)

    lane = lax.broadcasted_iota(I32, (tq, LANES), 1)
    low = lane < HEAD_DIM

    def step(masked):
        ones = jnp.ones((tk, LANES), BF16)
        if masked:
            causal = (lax.broadcasted_iota(I32, (tq, tk), 0) >= lax.broadcasted_iota(I32, (tq, tk), 1))
        for j in range(ATTN_WIDTH // LANES):
            sl = slice(j * LANES, (j + 1) * LANES)
            qp = q_ref[0, :, sl]
            kp = k_ref[0, :, sl]
            vx = jnp.concatenate([v_ref[0, :, sl], ones], axis=1)
            zero = jnp.zeros_like(qp)
            alphas = []
            pvs = []
            for t in range(2):
                hd = 2 * j + t
                qh = jnp.where(low, qp, zero) if t == 0 else jnp.where(low, zero, qp)
                decay = (fq_ref[0, hd:hd + 1, 0:1] - fk_ref[0, hd:hd + 1, :]) * LOG2E
                s = _dot_nt(qh, kp) + decay
                if masked:
                    s = jnp.where(causal, s, -jnp.inf)
                m_prev = m_sc[hd]
                m_new = jnp.maximum(m_prev, jnp.max(s, axis=-1, keepdims=True))
                alpha = jnp.exp2(m_prev - m_new)
                p = jnp.concatenate([jnp.exp2(s[:, c * LANES:(c + 1) * LANES] - m_new)
                                     for c in range(tk // LANES)], axis=1)
                pv = _dot(p.astype(BF16), vx)
                l_sc[hd] = alpha * l_sc[hd] + pv[:, LANES:]
                m_sc[hd] = m_new
                alphas.append(alpha)
                pvs.append(pv[:, :LANES])
            acc_sc[j] = (acc_sc[j] * jnp.where(low, alphas[0], alphas[1])
                         + jnp.where(low, pvs[0], pvs[1]))

    @pl.when(ki < qi)
    def _():
        step(False)

    @pl.when(ki == qi)
    def _():
        step(True)
        for j in range(ATTN_WIDTH // LANES):
            inv = jnp.where(low, 1.0 / l_sc[2 * j], 1.0 / l_sc[2 * j + 1])
            o_ref[0, :, j * LANES:(j + 1) * LANES] = (acc_sc[j] * inv).astype(o_ref.dtype)


def _attn_prompt(q, k, v, fr, tq):
    b, s, w = q.shape
    nq = s // tq
    pairs = [(qi, ki) for qi in range(nq) for ki in range(qi + 1)]
    qi_arr = jnp.asarray([p[0] for p in pairs], I32)
    ki_arr = jnp.asarray([p[1] for p in pairs], I32)
    qmap = lambda bi, p, qa, ka: (bi, qa[p], 0)
    kmap = lambda bi, p, qa, ka: (bi, ka[p], 0)
    grid_spec = pltpu.PrefetchScalarGridSpec(
        num_scalar_prefetch=2,
        grid=(b, len(pairs)),
        in_specs=[pl.BlockSpec((1, tq, w), qmap),
                  pl.BlockSpec((1, tq, w), kmap),
                  pl.BlockSpec((1, tq, w), kmap),
                  pl.BlockSpec((1, N_HEADS, tq), lambda bi, p, qa, ka: (bi, 0, qa[p])),
                  pl.BlockSpec((1, N_HEADS, tq), lambda bi, p, qa, ka: (bi, 0, ka[p]))],
        out_specs=pl.BlockSpec((1, tq, w), qmap),
        scratch_shapes=[pltpu.VMEM((N_HEADS, tq, LANES), F32), pltpu.VMEM((N_HEADS, tq, LANES), F32),
                        pltpu.VMEM((w // LANES, tq, LANES), F32)],
    )
    return pl.pallas_call(
        _attn_kernel,
        grid_spec=grid_spec,
        out_shape=jax.ShapeDtypeStruct((b, s, w), BF16),
        compiler_params=_params("arbitrary", "arbitrary"),
        name="attn_prompt",
    )(qi_arr, ki_arr, q, k, v, fr, fr)


def _rows_to_tile(rows):
    sub = lax.broadcasted_iota(I32, (N_HEADS, PAGE), 0)
    tile = jnp.zeros((N_HEADS, PAGE), F32)
    for h, r in enumerate(rows):
        tile = jnp.where(sub == h, r, tile)
    return tile


def _attn_decode_kernel(pt_ref, q_ref, kn_ref, vn_ref, lfn_ref, u_ref, *refs):
    n = PAGES_PER_STEP
    lf_refs, k_refs, v_refs = refs[:n], refs[n:2 * n], refs[2 * n:3 * n]
    o_ref, m_sc, l_sc, acc_sc, carry_sc = refs[3 * n:]
    j = pl.program_id(1)
    nj = pl.num_programs(1)
    heads = range(N_HEADS)

    @pl.when(j == 0)
    def _():
        m_sc[...] = jnp.full_like(m_sc, -jnp.inf)
        l_sc[...] = jnp.zeros_like(l_sc)
        acc_sc[...] = jnp.zeros_like(acc_sc)
        carry_sc[...] = jnp.zeros_like(carry_sc)

    lfn = lfn_ref[0]
    lf_all = jnp.concatenate([lf_refs[i][0, 0] for i in range(n)], axis=0)
    later_all = _dot3_right(lf_all, u_ref[...])
    carry = carry_sc[...]
    decays = []
    for i in range(n):
        sl = slice(i * N_HEADS, (i + 1) * N_HEADS)
        decays.append((later_all[sl] + carry + lfn) * LOG2E)
        carry = carry + jnp.sum(lf_all[sl], axis=-1, keepdims=True)
    carry_sc[...] = carry

    rows = [[None] * N_HEADS for _ in range(n)]
    for h in heads:
        qh = q_ref[0, h]
        for i in range(n):
            rows[i][h] = jnp.sum(k_refs[i][0, 0, h] * qh, axis=0, keepdims=True)
    s = [_rows_to_tile(rows[i]) + decays[i] for i in range(n)]
    m = m_sc[...]
    m_new = m
    for i in range(n):
        m_new = jnp.maximum(m_new, jnp.max(s[i], axis=-1, keepdims=True))
    alpha = jnp.exp2(m - m_new)
    p = [jnp.exp2(s[i] - m_new) for i in range(n)]
    l = alpha * l_sc[...]
    for i in range(n):
        l = l + jnp.sum(p[i], axis=-1, keepdims=True)
    m_sc[...] = m_new
    l_sc[...] = l
    alpha_rep = jnp.broadcast_to(alpha, (N_HEADS, PAGE))
    for h in heads:
        a = acc_sc[h] * alpha_rep[h:h + 1, :]
        for i in range(n):
            a = a + p[i][h:h + 1, :] * v_refs[i][0, 0, h]
        acc_sc[h] = a

    @pl.when(j == nj - 1)
    def _():
        s_n = _rows_to_tile([jnp.sum(q_ref[0, h] * kn_ref[0, h], axis=0, keepdims=True) for h in heads])
        m_rep = jnp.broadcast_to(m_new, (N_HEADS, PAGE))
        m_fin = jnp.maximum(m_rep, s_n)
        a_fin = jnp.exp2(m_rep - m_fin)
        p_n = jnp.exp2(s_n - m_fin)
        inv = 1.0 / (a_fin * jnp.broadcast_to(l, (N_HEADS, PAGE)) + p_n)
        for h in heads:
            total = jnp.broadcast_to(jnp.sum(acc_sc[h], axis=-1, keepdims=True), (HEAD_DIM, PAGE))
            o_ref[0, h] = (a_fin[h:h + 1, :] * total + p_n[h:h + 1, :] * vn_ref[0, h]) * inv[h:h + 1, :]


def _attn_decode(page_table, q_rep, k_new_rep, v_new_rep, lf_new, cache_lf_t, cache_k_t, cache_v_t):
    nb, n_pages = page_table.shape
    n = PAGES_PER_STEP
    tok3 = lambda b, j, pt: (b, 0, 0)
    tok4 = lambda b, j, pt: (b, 0, 0, 0)

    def page(i, rank):
        def index_map(b, j, pt):
            return (0, pt[b * n_pages + n_pages - 1 - (j * n + i)]) + (0,) * (rank - 2)
        return index_map

    lane = jnp.arange(PAGE)
    later = (lane[:, None] > lane[None, :]).astype(BF16)
    rep = pl.BlockSpec((1, N_HEADS, HEAD_DIM, PAGE), tok4)
    grid_spec = pltpu.PrefetchScalarGridSpec(
        num_scalar_prefetch=1,
        grid=(nb, n_pages // n),
        in_specs=[rep, rep, rep,
                  pl.BlockSpec((1, N_HEADS, 1), tok3),
                  pl.BlockSpec((PAGE, PAGE), lambda b, j, pt: (0, 0))]
                 + [pl.BlockSpec((1, 1, N_HEADS, PAGE), page(i, 4)) for i in range(n)]
                 + [pl.BlockSpec((1, 1, N_HEADS, HEAD_DIM, PAGE), page(i, 5)) for i in range(n)]
                 + [pl.BlockSpec((1, 1, N_HEADS, HEAD_DIM, PAGE), page(i, 5)) for i in range(n)],
        out_specs=rep,
        scratch_shapes=[pltpu.VMEM((N_HEADS, 1), F32), pltpu.VMEM((N_HEADS, 1), F32),
                        pltpu.VMEM((N_HEADS, HEAD_DIM, PAGE), F32), pltpu.VMEM((N_HEADS, 1), F32)],
    )
    return pl.pallas_call(
        _attn_decode_kernel,
        grid_spec=grid_spec,
        out_shape=jax.ShapeDtypeStruct((nb, N_HEADS, HEAD_DIM, PAGE), F32),
        compiler_params=_params("arbitrary", "arbitrary"),
        name="attn_decode",
    )(page_table.reshape(-1), q_rep, k_new_rep, v_new_rep, lf_new, later,
      *([cache_lf_t] * n), *([cache_k_t] * n), *([cache_v_t] * n))


def _mix_out_kernel(x_ref, oa_ref, sga_ref, mb_ref, mod_ref, wba_ref, wo_ref, gffn_ref,
                    wrh_ref, wrl_ref, wsg_ref, wsu_ref, wsd_ref,
                    h2_ref, lg_ref, base_ref):
    d = D_MODEL
    x = x_ref[0]
    gate1 = mod_ref[0, :, 2 * d:3 * d]
    shift2 = mod_ref[0, :, 3 * d:4 * d]
    scale2 = mod_ref[0, :, 4 * d:5 * d]
    gate2 = mod_ref[0, :, 5 * d:6 * d]
    merged = sga_ref[0].astype(F32) * _dot(oa_ref[0], wba_ref[...]) + mb_ref[0].astype(F32)
    x1 = x + gate1 * _dot(merged.astype(BF16), wo_ref[...])
    ms = jnp.mean(x1 * x1, axis=-1, keepdims=True)
    h2 = x1 * lax.rsqrt(ms + NORM_EPS) * gffn_ref[...] * (1.0 + scale2) + shift2
    h2_ref[0] = _pack_halves(h2)
    hb = h2.astype(BF16)
    hl = (h2 - hb.astype(F32)).astype(BF16)
    lg_ref[0] = _dot(hb, wrh_ref[...]) + (_dot(hb, wrl_ref[...]) + _dot(hl, wrh_ref[...]))
    g = _dot(hb, wsg_ref[...])
    u = _dot(hb, wsu_ref[...])
    a = (g * _sigmoid(g) * u).astype(BF16)
    base_ref[0] = x1 + gate2 * _dot(a, wsd_ref[...])


def _mix_out(x, oa, sga, mb, mod, wts, tl):
    bx, l, d = x.shape
    tlm = tl if mod.shape[1] > 1 else 1
    row = lambda b, i: (b, i, 0)
    mod_map = row if tlm > 1 else (lambda b, i: (b, 0, 0))
    names = ("wba", "wo", "gffn", "wrh", "wrl", "wsg", "wsu", "wsd")
    consts = [wts[n] for n in names]
    return pl.pallas_call(
        _mix_out_kernel,
        grid=(bx, l // tl),
        in_specs=[pl.BlockSpec((1, tl, d), row), pl.BlockSpec((1, tl, ATTN_WIDTH), row),
                  pl.BlockSpec((1, tl, d), row), pl.BlockSpec((1, tl, d), row),
                  pl.BlockSpec((1, tlm, 6 * d), mod_map)] + [_const_spec(c.shape) for c in consts],
        out_specs=[pl.BlockSpec((1, tl, d // 2), row), pl.BlockSpec((1, tl, LANES), row),
                   pl.BlockSpec((1, tl, d), row)],
        out_shape=[jax.ShapeDtypeStruct((bx, l, d // 2), U32), jax.ShapeDtypeStruct((bx, l, LANES), F32),
                   jax.ShapeDtypeStruct((bx, l, d), F32)],
        compiler_params=_params("arbitrary", "arbitrary"),
        name="mix_out",
    )(x, oa, sga, mb, mod, *consts)


def _route_kernel(lg_ref, b_ref, before_ref, lower_ref, w_ref, r_ref, cnt_ref):
    tt = lg_ref.shape[1]
    epg = EXPERTS_PER_GROUP
    ninf = -jnp.inf
    iota = lax.broadcasted_iota(I32, (epg, tt), 0)
    sc = []
    biased = []
    gscore = []
    for g in range(N_EXPERT_GROUPS):
        s = _sigmoid(lg_ref[g * epg:(g + 1) * epg, :])
        bz = s + b_ref[g * epg:(g + 1) * epg, :]
        m1 = jnp.max(bz, axis=0, keepdims=True)
        first = jnp.min(jnp.where(bz == m1, iota, epg), axis=0, keepdims=True)
        m2 = jnp.max(jnp.where(iota == first, ninf, bz), axis=0, keepdims=True)
        sc.append(s)
        biased.append(bz)
        gscore.append(m1 + m2)
    cand = []
    for g in range(N_EXPERT_GROUPS):
        rank = jnp.zeros((1, tt), I32)
        for o in range(N_EXPERT_GROUPS):
            if o == g:
                continue
            beats = (gscore[o] >= gscore[g]) if o < g else (gscore[o] > gscore[g])
            rank = rank + beats.astype(I32)
        cand.append(jnp.where(rank < TOP_K_GROUPS, biased[g], ninf))
    ws = []
    picks = []
    for k in range(TOP_K):
        mx = cand[0]
        for g in range(1, N_EXPERT_GROUPS):
            mx = jnp.maximum(mx, cand[g])
        mx = jnp.max(mx, axis=0, keepdims=True)
        fi = jnp.where(cand[0] == mx, iota, N_EXPERTS)
        for g in range(1, N_EXPERT_GROUPS):
            fi = jnp.minimum(fi, jnp.where(cand[g] == mx, iota + g * epg, N_EXPERTS))
        fi = jnp.min(fi, axis=0, keepdims=True)
        wk = jnp.zeros((epg, tt), F32)
        for g in range(N_EXPERT_GROUPS):
            hit = (iota + g * epg) == fi
            wk = wk + jnp.where(hit, sc[g], 0.0)
            cand[g] = jnp.where(hit, ninf, cand[g])
        picks.append(fi)
        ws.append(jnp.sum(wk, axis=0, keepdims=True))
    tot = ws[0]
    for k in range(1, TOP_K):
        tot = tot + ws[k]
    for k in range(TOP_K):
        w_ref[k:k + 1, :] = ws[k] / tot * ROUTED_SCALE

    chosen = []
    for g in range(N_EXPERT_GROUPS):
        sel = jnp.zeros((epg, tt), F32)
        for k in range(TOP_K):
            sel = sel + jnp.where((iota + g * epg) == picks[k], 1.0, 0.0)
        chosen.append(sel)
    chosen = jnp.concatenate(chosen, axis=0)
    earlier = _dot(chosen.astype(BF16), before_ref[...])
    run = jnp.floor((jnp.sum(chosen, axis=1, keepdims=True) + (SEG_ALIGN - 1)) * (1.0 / SEG_ALIGN)) * SEG_ALIGN
    run_rep = jnp.broadcast_to(run, (N_EXPERTS, LANES))
    cnt_ref[0] = run_rep
    start = _dot(lower_ref[...], run_rep.astype(BF16))[:, 0:1]
    local = earlier + start
    for k in range(TOP_K):
        ck = jnp.zeros((epg, tt), F32)
        for g in range(N_EXPERT_GROUPS):
            ck = ck + jnp.where((iota + g * epg) == picks[k], local[g * epg:(g + 1) * epg, :], 0.0)
        r_ref[k:k + 1, :] = jnp.sum(ck, axis=0, keepdims=True).astype(I32)


def _route(logits_t, b_router, tt):
    t = logits_t.shape[1]
    r = jnp.arange(tt)
    before = (r[:, None] < r[None, :]).astype(BF16)
    e = jnp.arange(N_EXPERTS)
    lower = (e[None, :] < e[:, None]).astype(BF16)
    col = lambda i: (0, i)
    return pl.pallas_call(
        _route_kernel,
        grid=(t // tt,),
        in_specs=[pl.BlockSpec((N_EXPERTS, tt), col),
                  pl.BlockSpec((N_EXPERTS, 1), lambda i: (0, 0)),
                  pl.BlockSpec((tt, tt), lambda i: (0, 0)),
                  pl.BlockSpec((N_EXPERTS, N_EXPERTS), lambda i: (0, 0))],
        out_specs=[pl.BlockSpec((TOP_K, tt), col), pl.BlockSpec((TOP_K, tt), col),
                   pl.BlockSpec((1, N_EXPERTS, LANES), lambda i: (i, 0, 0))],
        out_shape=[jax.ShapeDtypeStruct((TOP_K, t), F32), jax.ShapeDtypeStruct((TOP_K, t), I32),
                   jax.ShapeDtypeStruct((t // tt, N_EXPERTS, LANES), F32)],
        compiler_params=_params("arbitrary"),
        name="route",
    )(logits_t, b_router.reshape(N_EXPERTS, 1), before, lower)


def _for_groups(count, fn):
    bulk = count // GROUP_UNROLL

    def many(q, c):
        for u in range(GROUP_UNROLL):
            fn(q * GROUP_UNROLL + u)
        return c

    def one(g, c):
        fn(g)
        return c

    lax.fori_loop(0, bulk, many, 0)
    lax.fori_loop(bulk * GROUP_UNROLL, count, one, 0)


def _local_sort_kernel(to_ref, r_ref, h_ref, xs_hbm, buf, sem):
    tt = h_ref.shape[0]
    h = jnp.concatenate(_unpack_halves(h_ref[...]), axis=1).astype(BF16)
    for c in range(buf.shape[0] // SORT_CHUNK):
        rows = c * SORT_CHUNK + lax.broadcasted_iota(I32, (SORT_CHUNK, tt), 0)
        hit = rows == r_ref[0:1, :]
        for k in range(1, TOP_K):
            hit = hit | (rows == r_ref[k:k + 1, :])
        place = jnp.where(hit, 1.0, 0.0).astype(BF16)
        buf[c * SORT_CHUNK:(c + 1) * SORT_CHUNK, :] = _pack_halves(_dot(place, h))

    def copy(g):
        return pltpu.make_async_copy(buf.at[pl.ds(pl.multiple_of(g * SEG_ALIGN, SEG_ALIGN), SEG_ALIGN)],
                                     xs_hbm.at[pl.ds(pl.multiple_of(to_ref[0, 0, g], SEG_ALIGN), SEG_ALIGN)],
                                     sem.at[0])

    used = to_ref[0, 0, buf.shape[0] // SEG_ALIGN]
    _for_groups(used, lambda g: copy(g).start())
    _for_groups(used, lambda g: copy(g).wait())


def _local_rows(tt):
    pad = N_EXPERTS * SEG_ALIGN
    return -(-(tt * TOP_K + pad) // SORT_CHUNK) * SORT_CHUNK


def _local_sort(to_expert, r_t, h2, tt, out_rows):
    t, dp = h2.shape
    rows = _local_rows(tt)
    n = t // tt
    return pl.pallas_call(
        _local_sort_kernel,
        grid=(n,),
        in_specs=[pl.BlockSpec((1, 1, rows // SEG_ALIGN + 1), lambda i: (i, 0, 0), memory_space=pltpu.SMEM),
                  pl.BlockSpec((TOP_K, tt), lambda i: (0, i)), pl.BlockSpec((tt, dp), lambda i: (i, 0))],
        out_specs=pl.BlockSpec(memory_space=pl.ANY),
        out_shape=jax.ShapeDtypeStruct((out_rows, dp), U32),
        scratch_shapes=[pltpu.VMEM((rows, dp), U32), pltpu.SemaphoreType.DMA((1,))],
        compiler_params=_params("arbitrary", row_dma=True),
        name="moe_local_sort",
    )(to_expert.reshape(n, 1, rows // SEG_ALIGN + 1), r_t, h2)


VISIT_ACTIVE = 1
VISIT_FIRST = 2


def _moe_kernel(vt_ref, ve_ref, vf_ref, gs_ref, ge_ref, x_ref, wg_ref, wu_ref, wd_ref, o_ref):
    v = pl.program_id(0)
    tm = x_ref.shape[0]
    flags = vf_ref[v]

    @pl.when(flags >= VISIT_ACTIVE)
    def _():
        e = ve_ref[v]
        x = jnp.concatenate(_unpack_halves(x_ref[...]), axis=1).astype(BF16)
        g = _dot(x, wg_ref[0].astype(BF16))
        u = _dot(x, wu_ref[0].astype(BF16))
        a = (g * _sigmoid(g) * u).astype(BF16)
        res = _pack_halves(_dot(a, wd_ref[0].astype(BF16)))
        row = vt_ref[v] * tm + lax.broadcasted_iota(I32, (tm, 1), 0)
        mine = (row >= gs_ref[e]) & (row < ge_ref[e])

        @pl.when(flags >= VISIT_FIRST)
        def _():
            o_ref[...] = jnp.where(mine, res, jnp.zeros_like(res))

        @pl.when(flags < VISIT_FIRST)
        def _():
            o_ref[...] = jnp.where(mine, res, o_ref[...])


def _moe(plan, xs, w_gate, w_up, w_down, tm):
    visit_tile, visit_e, visit_flags, gs, ge = plan
    n_visits = visit_tile.shape[0]
    rows, dp = xs.shape
    d = 2 * dp
    tile = lambda v, vt, ve, vf, s, e: (vt[v], 0)
    expert = lambda v, vt, ve, vf, s, e: (ve[v], 0, 0)
    grid_spec = pltpu.PrefetchScalarGridSpec(
        num_scalar_prefetch=5,
        grid=(n_visits,),
        in_specs=[pl.BlockSpec((tm, dp), tile),
                  pl.BlockSpec((1, d, D_EXPERT), expert),
                  pl.BlockSpec((1, d, D_EXPERT), expert),
                  pl.BlockSpec((1, D_EXPERT, d), expert)],
        out_specs=pl.BlockSpec((tm, dp), tile),
    )
    return pl.pallas_call(
        _moe_kernel,
        grid_spec=grid_spec,
        out_shape=jax.ShapeDtypeStruct((rows, dp), U32),
        compiler_params=_params("arbitrary"),
        name="moe_experts",
    )(visit_tile, visit_e, visit_flags, gs, ge, xs, w_gate, w_up, w_down)


def _local_combine_kernel(fr_ref, frn_ref, eo_hbm, r_ref, w_ref, base_ref, mod_ref, y_ref, buf, sem):
    tt = base_ref.shape[1]
    half = D_MODEL // 2
    i = pl.program_id(0) * pl.num_programs(1) + pl.program_id(1)
    n = pl.num_programs(0) * pl.num_programs(1)
    slot = i % 2
    groups = buf.shape[1] // SEG_ALIGN

    def copy(tab_ref, s, g):
        return pltpu.make_async_copy(eo_hbm.at[pl.ds(pl.multiple_of(tab_ref[0, 0, g], SEG_ALIGN), SEG_ALIGN)],
                                     buf.at[s, pl.ds(pl.multiple_of(g * SEG_ALIGN, SEG_ALIGN), SEG_ALIGN)],
                                     sem.at[s])

    def issue(tab_ref, s):
        _for_groups(tab_ref[0, 0, groups], lambda g: copy(tab_ref, s, g).start())

    @pl.when(i == 0)
    def _():
        buf[...] = jnp.zeros_like(buf)
        issue(fr_ref, 0)

    @pl.when(i + 1 < n)
    def _():
        issue(frn_ref, 1 - slot)

    _for_groups(fr_ref[0, 0, groups], lambda g: copy(fr_ref, slot, g).wait())

    e_ref = buf.at[slot]
    lo = jnp.zeros((tt, half), F32)
    hi = jnp.zeros((tt, half), F32)
    for c in range(e_ref.shape[0] // SORT_CHUNK):
        cols = c * SORT_CHUNK + lax.broadcasted_iota(I32, (tt, SORT_CHUNK), 1)
        wm = jnp.zeros((tt, SORT_CHUNK), F32)
        for k in range(TOP_K):
            wm = jnp.where(cols == r_ref[0, :, k:k + 1], w_ref[0, :, k:k + 1], wm)
        wh = wm.astype(BF16)
        wl = (wm - wh.astype(F32)).astype(BF16)
        e_lo, e_hi = _unpack_halves(e_ref[c * SORT_CHUNK:(c + 1) * SORT_CHUNK, :])
        e_lo, e_hi = e_lo.astype(BF16), e_hi.astype(BF16)
        lo = lo + (_dot(wh, e_lo) + _dot(wl, e_lo))
        hi = hi + (_dot(wh, e_hi) + _dot(wl, e_hi))
    gate2 = mod_ref[0, :, 5 * D_MODEL:6 * D_MODEL]
    y_ref[0, :, :half] = base_ref[0, :, :half] + gate2[:, :half] * lo
    y_ref[0, :, half:] = base_ref[0, :, half:] + gate2[:, half:] * hi


def _local_combine(from_expert, eo, r_tok, w_tok, base, mod, tt):
    bx, l, d = base.shape
    nl = l // tt
    n = bx * nl
    rows = _local_rows(tt)
    groups = rows // SEG_ALIGN
    tab = from_expert.reshape(n, 1, groups + 1)
    tlm = tt if mod.shape[1] > 1 else 1
    row = lambda b, i: (b, i, 0)
    mod_map = row if tlm > 1 else (lambda b, i: (b, 0, 0))
    return pl.pallas_call(
        _local_combine_kernel,
        grid=(bx, nl),
        in_specs=[pl.BlockSpec((1, 1, groups + 1), lambda b, i: (b * nl + i, 0, 0), memory_space=pltpu.SMEM),
                  pl.BlockSpec((1, 1, groups + 1), lambda b, i: (jnp.minimum(b * nl + i + 1, n - 1), 0, 0),
                               memory_space=pltpu.SMEM),
                  pl.BlockSpec(memory_space=pl.ANY),
                  pl.BlockSpec((1, tt, TOP_K), row), pl.BlockSpec((1, tt, TOP_K), row),
                  pl.BlockSpec((1, tt, d), row),
                  pl.BlockSpec((1, tlm, 6 * d), mod_map)],
        out_specs=pl.BlockSpec((1, tt, d), row),
        out_shape=jax.ShapeDtypeStruct((bx, l, d), F32),
        scratch_shapes=[pltpu.VMEM((2, rows, d // 2), U32), pltpu.SemaphoreType.DMA((2,))],
        compiler_params=_params("arbitrary", "arbitrary", row_dma=True),
        name="moe_local_combine",
    )(tab, tab, eo, r_tok, w_tok, base, mod)


def _moe_plan(runs, tt, tm):
    n_tok_tiles = runs.shape[0]
    runs = runs.astype(I32)
    local_start = jnp.cumsum(runs, axis=1) - runs
    totals = jnp.sum(runs, axis=0)
    ge = jnp.cumsum(totals)
    gs = ge - totals
    expert_major = gs[None, :] + jnp.cumsum(runs, axis=0) - runs
    local_row = jnp.arange(_local_rows(tt) // SEG_ALIGN, dtype=I32) * SEG_ALIGN
    run_of = jnp.sum((local_start + runs)[:, None, :] <= local_row[None, :, None], axis=2)
    in_run = run_of[:, :, None] == jnp.arange(N_EXPERTS, dtype=I32)[None, None, :]
    shift = jnp.sum(jnp.where(in_run, (expert_major - local_start)[:, None, :], 0), axis=2)
    table = jnp.where(run_of < N_EXPERTS, local_row[None, :] + shift, -1).astype(I32)
    table = jnp.concatenate([table, jnp.sum(runs, axis=1, keepdims=True) // SEG_ALIGN], axis=1)
    max_rows = -(-(n_tok_tiles * (tt * TOP_K + N_EXPERTS * (SEG_ALIGN - 1))) // tm) * tm
    n_tiles = max_rows // tm
    n_visits = n_tiles + N_EXPERTS
    experts = jnp.arange(N_EXPERTS, dtype=I32)
    first_tile = gs // tm
    n_vis = jnp.where(totals > 0, (ge - 1) // tm - first_tile + 1, 0)
    vend = jnp.cumsum(n_vis)
    vstart = vend - n_vis
    v = jnp.arange(n_visits, dtype=I32)
    active = v < vend[-1]
    e_of_v = jnp.minimum(jnp.sum(vend[None, :] <= v[:, None], axis=1), N_EXPERTS - 1).astype(I32)
    onehot = e_of_v[:, None] == experts[None, :]
    pick = lambda table: jnp.sum(jnp.where(onehot, table[None, :], 0), axis=1)
    last_tile = jnp.maximum(ge[-1] - 1, 0) // tm
    tile_of_v = jnp.where(active, pick(first_tile) + v - pick(vstart), last_tile).astype(I32)
    prev_tile = jnp.concatenate([jnp.full((1,), -1, I32), tile_of_v[:-1]])
    flags = jnp.where(active, VISIT_ACTIVE + VISIT_FIRST * (tile_of_v != prev_tile), 0).astype(I32)
    return table, (tile_of_v, e_of_v, flags, gs, ge), max_rows


def _prep_weights(w_in, b_forget, g_q, g_k, g_vnorm, b_vnorm, g_norm_mix, g_norm_ffn,
                  w_branch_a, w_branch_b, w_out, w_router, w_sh_gate, w_sh_up, w_sh_down):
    aw, sw, d = ATTN_WIDTH, SGU_WIDTH, D_MODEL
    o = 3 * aw + N_HEADS
    wf = jnp.zeros((d, LANES), F32).at[:, :N_HEADS].set(w_in[:, 3 * aw:o])
    bf = jnp.zeros((1, LANES), F32).at[0, :N_HEADS].set(b_forget)
    lane = jnp.arange(aw)
    bd = (lane[:, None] // HEAD_DIM == lane[None, :] // HEAD_DIM).astype(BF16)
    wr = jnp.zeros((d, LANES), F32).at[:, :N_EXPERTS].set(w_router)
    wrh = wr.astype(BF16)
    wrl = (wr - wrh.astype(F32)).astype(BF16)
    return dict(
        gmix=g_norm_mix.reshape(1, d), gffn=g_norm_ffn.reshape(1, d),
        wq=w_in[:, 0:aw].astype(BF16), wk=w_in[:, aw:2 * aw].astype(BF16),
        wv=w_in[:, 2 * aw:3 * aw].astype(BF16), wf=wf.astype(BF16), bf=bf,
        wu=w_in[:, o:o + sw].astype(BF16), wvg=w_in[:, o + sw:o + 2 * sw].astype(BF16),
        wga=w_in[:, o + 2 * sw:o + 2 * sw + d].astype(BF16),
        wgb=w_in[:, o + 2 * sw + d:o + 2 * sw + 2 * d].astype(BF16),
        gq=jnp.tile(g_q, N_HEADS).reshape(1, aw), gk=jnp.tile(g_k, N_HEADS).reshape(1, aw),
        gvn=g_vnorm.reshape(1, sw), bvn=b_vnorm.reshape(1, sw), bd=bd,
        wbb=w_branch_b.astype(BF16), wba=w_branch_a.astype(BF16), wo=w_out.astype(BF16),
        wrh=wrh, wrl=wrl, wsg=w_sh_gate.astype(BF16), wsu=w_sh_up.astype(BF16),
        wsd=w_sh_down.astype(BF16))


def _spatial_weights(w_spatial, b_spatial, rows_are_sequences, tl):
    if rows_are_sequences:
        wsp = w_spatial[:, 0, 0][:, None, None] * jnp.eye(CHUNK, dtype=F32)[None]
        b = jnp.broadcast_to(b_spatial[:, 0:1], (SGU_GROUPS, CHUNK))
    else:
        wsp = jnp.where(jnp.tril(jnp.ones((CHUNK, CHUNK), bool)), w_spatial, 0)
        b = b_spatial
    half = LANES // 2
    bsp = jnp.repeat(b.reshape(SGU_GROUPS // 2, 2, CHUNK), half, axis=1)
    bsp = bsp.transpose(0, 2, 1)
    r = jnp.arange(tl)
    ltri = (r[:, None] >= r[None, :]).astype(BF16)
    return dict(wsp=wsp.astype(BF16), bsp=bsp, ltri=ltri)


def _layer(x, mod, attend, wts, w_spatial, b_spatial, b_router, w_exp_gate, w_exp_up, w_exp_down,
           rows_are_sequences, tl, tm, tt):
    bx, l, d = x.shape
    wts = dict(wts, **_spatial_weights(w_spatial, b_spatial, rows_are_sequences, tl))
    q, kf, vf, kb, vb, lf, fc, mb, sga, vn = _mix_in(x, mod, wts, tl)
    oa = attend(q, kf, vf, kb, vb, lf, fc)
    h2, logits, base = _mix_out(x, oa, sga, mb, mod, wts, tl)
    t = bx * l
    w_t, r_t, runs = _route(logits.reshape(t, LANES).T, b_router, tt)
    table, plan, max_rows = _moe_plan(runs[:, :, 0], tt, tm)
    xs = _local_sort(table, r_t, h2.reshape(t, d // 2), tt, max_rows)
    eo = _moe(plan, xs, w_exp_gate, w_exp_up, w_exp_down, tm)
    y = _local_combine(table, eo, r_t.T.reshape(bx, l, TOP_K), w_t.T.reshape(bx, l, TOP_K), base, mod, tt)
    return y, kf, vf, lf, vn


def kernel(x_prompt, x_sample, c_prompt, c_sample, cache_k, cache_v, cache_logf, page_table, w_ada, b_ada, g_norm_mix, g_norm_ffn, w_in, b_forget, g_q, g_k, g_vnorm, b_vnorm, w_spatial, b_spatial, w_branch_a, w_branch_b, w_out, w_router, b_router, w_exp_gate, w_exp_up, w_exp_down, w_sh_gate, w_sh_up, w_sh_down):
    assert w_ada.shape[0] == 1, "one layer"
    b, s, d = x_prompt.shape
    nb = x_sample.shape[0]

    c_all = jnp.concatenate([c_prompt, c_sample], axis=0)
    pad = (-c_all.shape[0]) % 8
    c_all = jnp.pad(c_all, ((0, pad), (0, 0)))
    mod_all = _ada(c_all, w_ada[0], b_ada[0])
    mod_p = mod_all[:b].reshape(b, 1, 6 * d)
    mod_s = mod_all[b:b + nb].reshape(1, nb, 6 * d)

    wts = _prep_weights(w_in[0], b_forget[0], g_q[0], g_k[0], g_vnorm[0], b_vnorm[0], g_norm_mix[0],
                        g_norm_ffn[0], w_branch_a[0], w_branch_b[0], w_out[0], w_router[0],
                        w_sh_gate[0], w_sh_up[0], w_sh_down[0])
    experts = (w_exp_gate[0], w_exp_up[0], w_exp_down[0])

    tq = min(512, s)

    def attend_prompt(q, kf, vf, kb, vb, lf, fc):
        return _attn_prompt(q, kb, vb, fc.transpose(0, 2, 1), tq)

    def attend_sample(q, kf, vf, kb, vb, lf, fc):
        def lane_rep(a):
            a = a.astype(F32).reshape(nb, N_HEADS, HEAD_DIM, 1)
            return jnp.broadcast_to(a, (nb, N_HEADS, HEAD_DIM, PAGE))

        o = _attn_decode(page_table, lane_rep(q), lane_rep(kf), lane_rep(vf),
                         lf.reshape(nb, N_HEADS, 1),
                         cache_logf.transpose(0, 1, 3, 2),
                         cache_k.transpose(0, 1, 3, 4, 2),
                         cache_v.transpose(0, 1, 3, 4, 2))
        return o[..., 0].reshape(1, nb, ATTN_WIDTH).astype(BF16)

    y_s, k_s, v_s, lf_s, vn_s = _layer(x_sample.reshape(1, nb, d), mod_s, attend_sample, wts,
                                       w_spatial[0], b_spatial[0], b_router[0], *experts,
                                       rows_are_sequences=True, tl=nb, tm=32, tt=nb)
    y_p, k_p, v_p, lf_p, _ = _layer(x_prompt, mod_p, attend_prompt, wts, w_spatial[0], b_spatial[0],
                                    b_router[0], *experts, rows_are_sequences=False,
                                    tl=min(512, s), tm=min(512, s), tt=min(256, s))
    hd5 = (1, b, s, N_HEADS, HEAD_DIM)
    sd5 = (1, nb, 1, N_HEADS, HEAD_DIM)
    return (y_p, y_s.reshape(nb, 1, d),
            k_p.reshape(hd5), v_p.reshape(hd5), lf_p.reshape(1, b, s, N_HEADS),
            k_s.reshape(sd5), v_s.reshape(sd5), lf_s.reshape(1, nb, 1, N_HEADS),
            vn_s.reshape(1, nb, 1, SGU_WIDTH))
```
